```python
import jax
import jax.numpy as jnp
from jax import lax
import numpy as np

D_MODEL = 2048
BATCH = 4
SEQ = 4096
DEPTH = 1
DEC_BATCH = 128
DEC_SEQ = 1
PAST_LEN = 16384
PAGE_SIZE = 128

H_MLA = 8
QK_NOPE = 128
QK_ROPE = 64
QK_HEAD = QK_NOPE + QK_ROPE
V_DIM = 128
Q_LORA = 512
KV_LORA = 256
D_MLA = H_MLA * V_DIM
H_FOX = 8
FOX_KV_HEADS = 2
FOX_GROUP = H_FOX // FOX_KV_HEADS
FOX_HD = 128
D_FOX = H_FOX * FOX_HD
D_MIX = D_MLA + D_FOX
FOX_BIAS_MEAN = 2.0

BLOCK_Q = 128
ROPE_THETA = 10000.0
EPS = 1e-6
MLA_SCALE = QK_HEAD ** -0.5
FOX_SCALE = FOX_HD ** -0.5
IN_WIDTHS = (Q_LORA, KV_LORA, QK_ROPE, D_MLA,
             D_FOX, FOX_KV_HEADS * FOX_HD, FOX_KV_HEADS * FOX_HD, H_FOX, D_FOX)
D_IN = sum(IN_WIDTHS)

kernel_name = 'hymba_mla_fox_adaln_step'


def rms_norm(x, g):
    xf = x.astype(jnp.float32)
    y = xf * lax.rsqrt(jnp.mean(xf * xf, axis=-1, keepdims=True) + EPS)
    return (y * g.astype(jnp.float32)).astype(x.dtype)


def rope_angles(pos):
    inv_freq = ROPE_THETA ** (-jnp.arange(0, QK_ROPE, 2, dtype=jnp.float32) / QK_ROPE)
    ang = pos.astype(jnp.float32)[:, None] * inv_freq[None, :]
    return jnp.cos(ang), jnp.sin(ang)


def apply_rope(x, cos, sin):
    shape = (cos.shape[0],) + (1,) * (x.ndim - 3) + (cos.shape[1],)
    cos, sin = cos.reshape(shape), sin.reshape(shape)
    x1, x2 = jnp.split(x.astype(jnp.float32), 2, axis=-1)
    return jnp.concatenate([x1 * cos - x2 * sin, x2 * cos + x1 * sin], axis=-1).astype(x.dtype)


def split_columns(u):
    parts, start = [], 0
    for width in IN_WIDTHS:
        parts.append(u[..., start:start + width])
        start += width
    return parts


def project_layer(x, c, pos, lp):
    b, s, _ = x.shape
    mod = jax.nn.silu(c) @ lp['w_ada'] + lp['b_ada']
    shift, scale, gate = jnp.split(mod[:, None, :], 3, axis=-1)
    h = rms_norm(x, lp['g_norm']) * (1 + scale) + shift
    u = h @ lp['w_in']
    q_a, kv_a, k_r, z_mla, f_q, f_k, f_v, f_lin, z_fox = split_columns(u)
    cos, sin = rope_angles(pos)
    q = (rms_norm(q_a, lp['g_qa']) @ lp['w_uq']).reshape(b, s, H_MLA, QK_HEAD)
    q_nope = rms_norm(q[..., :QK_NOPE], lp['g_qn_nope'])
    q_rope = apply_rope(rms_norm(q[..., QK_NOPE:], lp['g_qn_rope']), cos, sin)
    c_kv = rms_norm(kv_a, lp['g_kva'])
    k_rope = apply_rope(rms_norm(k_r, lp['g_kn_rope']), cos, sin)
    fq = rms_norm(f_q.reshape(b, s, H_FOX, FOX_HD), lp['g_fq'])
    fk = rms_norm(f_k.reshape(b, s, FOX_KV_HEADS, FOX_HD), lp['g_fk'])
    fv = f_v.reshape(b, s, FOX_KV_HEADS, FOX_HD)
    logf = jax.nn.log_sigmoid((f_lin + lp['b_f']).astype(jnp.float32))
    mix = (q_nope, q_rope, c_kv, k_rope, fq, fk, fv, logf)
    return mix, (z_mla, z_fox, gate)


def mla_keys_values(c_kv, lp):
    k_nope = rms_norm(jnp.einsum('...tc,chd->...thd', c_kv, lp['w_uk']), lp['g_kn_nope'])
    v = jnp.einsum('...tc,chd->...thd', c_kv, lp['w_uv'])
    return k_nope, v


def mla_attend(q_nope, q_rope, k_nope, k_rope, v, mask):
    s = (jnp.einsum('bqhd,bkhd->bhqk', q_nope, k_nope)
         + jnp.einsum('bqhr,bkr->bhqk', q_rope, k_rope))
    s = jnp.where(mask, s.astype(jnp.float32) * MLA_SCALE, -jnp.inf)
    p = jax.nn.softmax(s, axis=-1).astype(v.dtype)
    return jnp.einsum('bhqk,bkhd->bqhd', p, v)


def fox_attend(q, k, v, f_cum_q, f_cum_k, mask):
    b, nq = q.shape[:2]
    nk = k.shape[1]
    qg = q.reshape(b, nq, FOX_KV_HEADS, FOX_GROUP, FOX_HD)
    s = jnp.einsum('bqgjd,bkgd->bgjqk', qg, k).reshape(b, H_FOX, nq, nk)
    bias = jnp.transpose(f_cum_q, (0, 2, 1))[..., :, None] - jnp.transpose(f_cum_k, (0, 2, 1))[..., None, :]
    s = jnp.where(mask, s.astype(jnp.float32) * FOX_SCALE + bias, -jnp.inf)
    p = jax.nn.softmax(s, axis=-1).astype(v.dtype).reshape(b, FOX_KV_HEADS, FOX_GROUP, nq, nk)
    return jnp.einsum('bgjqk,bkgd->bqgjd', p, v).reshape(b, nq, H_FOX, FOX_HD)


def prompt_mixers(mix, lp):
    q_nope, q_rope, c_kv, k_rope, fq, fk, fv, logf = mix
    b, s = q_nope.shape[:2]
    k_nope, v_mla = mla_keys_values(c_kv, lp)
    f_cum = jnp.cumsum(logf, axis=1)
    kpos = jnp.arange(s)

    def block(i):
        q0 = i * BLOCK_Q
        take = lambda a: lax.dynamic_slice_in_dim(a, q0, BLOCK_Q, axis=1)
        mask = kpos[None, :] <= (q0 + jnp.arange(BLOCK_Q))[:, None]
        o_m = mla_attend(take(q_nope), take(q_rope), k_nope, k_rope, v_mla, mask)
        o_f = fox_attend(take(fq), fk, fv, take(f_cum), f_cum, mask)
        return o_m, o_f

    o_m, o_f = lax.map(block, jnp.arange(s // BLOCK_Q))
    o_m = jnp.moveaxis(o_m, 0, 1).reshape(b, s, D_MLA)
    o_f = jnp.moveaxis(o_f, 0, 1).reshape(b, s, D_FOX)
    return o_m, o_f


def sample_mixers(mix, l, page_table, cache_mla_ckv, cache_mla_krope, cache_fox_k, cache_fox_v,
                  cache_fox_logf, lp):
    t_new = mix[0].shape[1]
    past = page_table.shape[1] * cache_mla_ckv.shape[2]
    mask = jnp.arange(past + t_new)[None, :] <= (past + jnp.arange(t_new))[:, None]

    def gather(cache, row):
        rows = cache[l, row]
        return rows.reshape((past,) + rows.shape[2:])

    def one_seq(args):
        row, q_nope, q_rope, c_kv, k_rope, fq, fk, fv, logf = args
        ckv_all = jnp.concatenate([gather(cache_mla_ckv, row), c_kv], axis=0)[None]
        kr_all = jnp.concatenate([gather(cache_mla_krope, row), k_rope], axis=0)[None]
        fk_all = jnp.concatenate([gather(cache_fox_k, row), fk], axis=0)[None]
        fv_all = jnp.concatenate([gather(cache_fox_v, row), fv], axis=0)[None]
        logf_all = jnp.concatenate([gather(cache_fox_logf, row).astype(jnp.float32), logf], axis=0)[None]
        f_cum = jnp.cumsum(logf_all, axis=1)
        k_nope, v_mla = mla_keys_values(ckv_all, lp)
        o_m = mla_attend(q_nope[None], q_rope[None], k_nope, kr_all, v_mla, mask)[0]
        o_f = fox_attend(fq[None], fk_all, fv_all, f_cum[:, past:], f_cum, mask)[0]
        return o_m, o_f

    o_m, o_f = lax.map(one_seq, (page_table, *mix))
    n = page_table.shape[0]
    return o_m.reshape(n, t_new, D_MLA), o_f.reshape(n, t_new, D_FOX)


def finish(x, o_m, o_f, gates, w_out):
    z_mla, z_fox, gate = gates
    merged = jnp.concatenate([o_m * jax.nn.silu(z_mla), o_f * jax.nn.silu(z_fox)], axis=-1)
    return x + gate * (merged @ w_out)


def setup_inputs(seed: int = 0) -> dict:
    key = jax.random.key(seed)
    ks = jax.random.split(key, 32)
    n_pages = PAST_LEN // PAGE_SIZE
    n_used = DEC_BATCH * n_pages
    n_pool = (n_used * 5) // 4
    f32 = jnp.float32

    def nrm(k, shape, s=1.0):
        return s * jax.random.normal(k, shape, f32)

    def gain(k, n):
        return 1.0 + 0.1 * jax.random.normal(k, (DEPTH, n), f32)

    page_table = jax.random.permutation(ks[0], n_pool)[:n_used].reshape(DEC_BATCH, n_pages).astype(jnp.int32)
    return {
        'x_prompt': nrm(ks[1], (BATCH, SEQ, D_MODEL)),
        'x_sample': nrm(ks[2], (DEC_BATCH, DEC_SEQ, D_MODEL)),
        'c_prompt': nrm(ks[3], (BATCH, D_MODEL)),
        'c_sample': nrm(ks[4], (DEC_BATCH, D_MODEL)),
        'cache_mla_ckv': nrm(ks[5], (DEPTH, n_pool, PAGE_SIZE, KV_LORA)),
        'cache_mla_krope': nrm(ks[6], (DEPTH, n_pool, PAGE_SIZE, QK_ROPE)),
        'cache_fox_k': nrm(ks[7], (DEPTH, n_pool, PAGE_SIZE, FOX_KV_HEADS, FOX_HD)),
        'cache_fox_v': nrm(ks[8], (DEPTH, n_pool, PAGE_SIZE, FOX_KV_HEADS, FOX_HD)),
        'cache_fox_logf': jax.nn.log_sigmoid(FOX_BIAS_MEAN + nrm(ks[9], (DEPTH, n_pool, PAGE_SIZE, H_FOX))),
        'page_table': page_table,
        'g_norm': gain(ks[10], D_MODEL),
        'w_ada': nrm(ks[11], (DEPTH, D_MODEL, 3 * D_MODEL), 0.5 * D_MODEL ** -0.5),
        'b_ada': nrm(ks[12], (DEPTH, 3 * D_MODEL), 0.01),
        'w_in': nrm(ks[13], (DEPTH, D_MODEL, D_IN), D_MODEL ** -0.5),
        'b_f': FOX_BIAS_MEAN + nrm(ks[14], (DEPTH, H_FOX), 0.5),
        'g_qa': gain(ks[15], Q_LORA),
        'w_uq': nrm(ks[16], (DEPTH, Q_LORA, H_MLA * QK_HEAD), Q_LORA ** -0.5),
        'g_qn_nope': gain(ks[17], QK_NOPE),
        'g_qn_rope': gain(ks[18], QK_ROPE),
        'g_kva': gain(ks[19], KV_LORA),
        'g_kn_nope': gain(ks[20], QK_NOPE),
        'g_kn_rope': gain(ks[21], QK_ROPE),
        'w_uk': nrm(ks[22], (DEPTH, KV_LORA, H_MLA, QK_NOPE), KV_LORA ** -0.5),
        'w_uv': nrm(ks[23], (DEPTH, KV_LORA, H_MLA, V_DIM), KV_LORA ** -0.5),
        'g_fq': gain(ks[24], FOX_HD),
        'g_fk': gain(ks[25], FOX_HD),
        'w_out': nrm(ks[26], (DEPTH, D_MIX, D_MODEL), D_MIX ** -0.5),
    }


def reference(x_prompt, x_sample, c_prompt, c_sample, cache_mla_ckv, cache_mla_krope, cache_fox_k,
              cache_fox_v, cache_fox_logf, page_table, g_norm, w_ada, b_ada, w_in, b_f, g_qa, w_uq,
              g_qn_nope, g_qn_rope, g_kva, g_kn_nope, g_kn_rope, w_uk, w_uv, g_fq, g_fk, w_out):
    past = page_table.shape[1] * cache_mla_ckv.shape[2]
    pos_prompt = jnp.arange(x_prompt.shape[1])
    pos_sample = past + jnp.arange(x_sample.shape[1])
    ckv_p, kr_p, fk_p, fv_p, lf_p = [], [], [], [], []
    ckv_s, kr_s, fk_s, fv_s, lf_s = [], [], [], [], []
    xp, xs = x_prompt, x_sample
    for l in range(DEPTH):
        lp = {'g_norm': g_norm[l], 'w_ada': w_ada[l], 'b_ada': b_ada[l], 'w_in': w_in[l], 'b_f': b_f[l],
              'g_qa': g_qa[l], 'w_uq': w_uq[l], 'g_qn_nope': g_qn_nope[l], 'g_qn_rope': g_qn_rope[l],
              'g_kva': g_kva[l], 'g_kn_nope': g_kn_nope[l], 'g_kn_rope': g_kn_rope[l], 'w_uk': w_uk[l],
              'w_uv': w_uv[l], 'g_fq': g_fq[l], 'g_fk': g_fk[l]}
        mix_p, gates_p = project_layer(xp, c_prompt, pos_prompt, lp)
        o_m, o_f = prompt_mixers(mix_p, lp)
        xp = finish(xp, o_m, o_f, gates_p, w_out[l])
        mix_s, gates_s = project_layer(xs, c_sample, pos_sample, lp)
        o_m, o_f = sample_mixers(mix_s, l, page_table, cache_mla_ckv, cache_mla_krope, cache_fox_k,
                                 cache_fox_v, cache_fox_logf, lp)
        xs = finish(xs, o_m, o_f, gates_s, w_out[l])
        ckv_p.append(mix_p[2]); kr_p.append(mix_p[3]); fk_p.append(mix_p[5]); fv_p.append(mix_p[6]); lf_p.append(mix_p[7])
        ckv_s.append(mix_s[2]); kr_s.append(mix_s[3]); fk_s.append(mix_s[5]); fv_s.append(mix_s[6]); lf_s.append(mix_s[7])
    return (xp, xs,
            jnp.stack(ckv_p), jnp.stack(kr_p), jnp.stack(fk_p), jnp.stack(fv_p), jnp.stack(lf_p),
            jnp.stack(ckv_s), jnp.stack(kr_s), jnp.stack(fk_s), jnp.stack(fv_s), jnp.stack(lf_s))
```

```python
import functools

import jax
import jax.numpy as jnp
import numpy as np
from jax import lax
from jax.experimental import pallas as pl
from jax.experimental.pallas import tpu as pltpu

F32 = jnp.float32
BF16 = jnp.bfloat16

H_MLA = 8
QK_NOPE = 128
QK_ROPE = 64
QK_HEAD = QK_NOPE + QK_ROPE
QK_PAD = 256
V_DIM = 128
Q_LORA = 512
KV_LORA = 256
D_MLA = H_MLA * V_DIM
H_FOX = 8
FOX_KV_HEADS = 2
FOX_GROUP = H_FOX // FOX_KV_HEADS
FOX_HD = 128
D_FOX = H_FOX * FOX_HD
D_KV_FOX = FOX_KV_HEADS * FOX_HD
ROPE_THETA = 10000.0
EPS = 1e-6
MLA_SCALE = QK_HEAD ** -0.5
FOX_SCALE = FOX_HD ** -0.5
LANES = 128
MXU_DIM = 256
NEG_INF = -1e30

C_QA = 0
C_KVA = C_QA + Q_LORA
C_KR = C_KVA + KV_LORA
C_ZM = C_KR + 2 * QK_ROPE
C_FQ = C_ZM + D_MLA
C_FK = C_FQ + D_FOX
C_FV = C_FK + D_KV_FOX
C_ZF = C_FV + D_KV_FOX
C_FL = C_ZF + D_FOX
C_END = C_FL + LANES

VMEM_LIMIT = 56 * 1024 * 1024


def _params(*sem):
    return pltpu.CompilerParams(dimension_semantics=sem, vmem_limit_bytes=VMEM_LIMIT)


def _const_spec(shape):
    nd = len(shape)
    return pl.BlockSpec(shape, lambda *_: (0,) * nd, pipeline_mode=pl.Buffered(1))


def _ada_kernel(c_ref, w_ref, b_ref, o_ref):
    c = c_ref[...]
    a = (c / (1.0 + jnp.exp(-c))).astype(BF16)
    o_ref[...] = jnp.dot(a, w_ref[...].astype(BF16), preferred_element_type=F32) + b_ref[...]


def _ada(c_all, w_ada, b_ada):
    rows, d = c_all.shape
    n = w_ada.shape[1]
    tn = 512
    return pl.pallas_call(
        _ada_kernel,
        grid=(n // tn,),
        in_specs=[pl.BlockSpec((rows, d), lambda j: (0, 0)),
                  pl.BlockSpec((d, tn), lambda j: (0, j)),
                  pl.BlockSpec((1, tn), lambda j: (0, j))],
        out_specs=pl.BlockSpec((rows, tn), lambda j: (0, j)),
        out_shape=jax.ShapeDtypeStruct((rows, n), F32),
        compiler_params=_params("arbitrary"),
        name="ada",
    )(c_all, w_ada, b_ada)


def _seg_rms(x, b_ref, gain):
    ms = jnp.dot((x * x).astype(BF16), b_ref[...], preferred_element_type=F32)
    return x * lax.rsqrt(ms + EPS) * gain


def _silu(z):
    return z / (1.0 + jnp.exp(-z))


def _proj_kernel(x_ref, shift_ref, scale_ref, gnorm_ref, win_ref, wuq_ref, wukv_ref,
                 gqa_ref, gq_ref, gkva_ref, gkn_ref, gkr_ref, gfq_ref, gfk_ref, bf_ref,
                 bq_ref, b128_ref, bkr_ref, tabq_ref, tabk_ref,
                 qcat_o, ckv_o, kr_o, kn_o, v_o, krb_o, fq_o, fk_o, fkb_o, fv_o, fvb_o, lf_o, sz_o):
    x = x_ref[...]
    ms = jnp.mean(x * x, axis=-1, keepdims=True)
    xn = x * lax.rsqrt(ms + EPS) * gnorm_ref[...]
    h = xn * (1.0 + scale_ref[...]) + shift_ref[...]
    hb = h.astype(BF16)

    def proj(lo, hi):
        return jnp.dot(hb, win_ref[:, lo:hi], preferred_element_type=F32)

    lane = lax.broadcasted_iota(jnp.int32, (x.shape[0], LANES), 1)

    qa = proj(C_QA, C_KVA)
    qn = qa * lax.rsqrt(jnp.mean(qa * qa, axis=-1, keepdims=True) + EPS) * gqa_ref[...]
    q_raw = jnp.dot(qn.astype(BF16), wuq_ref[...], preferred_element_type=F32)
    tabq = tabq_ref[...]
    for hd in range(H_MLA):
        lo = hd * QK_PAD
        y = _seg_rms(q_raw[:, lo:lo + QK_PAD], bq_ref, gq_ref[:, lo:lo + QK_PAD])
        t = y[:, LANES:] * tabq
        rot = t + pltpu.roll(t, QK_ROPE, axis=1)
        qcat_o[:, lo:lo + LANES] = y[:, :LANES].astype(BF16)
        qcat_o[:, lo + LANES:lo + QK_PAD] = jnp.where(lane < QK_ROPE, rot, 0.0).astype(BF16)

    kva = proj(C_KVA, C_KR)
    ckv = kva * lax.rsqrt(jnp.mean(kva * kva, axis=-1, keepdims=True) + EPS) * gkva_ref[...]
    ckv_o[...] = ckv
    kv_raw = jnp.dot(ckv.astype(BF16), wukv_ref[...], preferred_element_type=F32)
    for c in range(D_MLA // MXU_DIM):
        lo = c * MXU_DIM
        kn_o[:, lo:lo + MXU_DIM] = _seg_rms(kv_raw[:, lo:lo + MXU_DIM], b128_ref,
                                            gkn_ref[:, lo:lo + MXU_DIM]).astype(BF16)
    v_o[...] = kv_raw[:, D_MLA:].astype(BF16)

    kr = _seg_rms(proj(C_KR, C_ZM), bkr_ref, gkr_ref[...]) * tabk_ref[...]
    kr = kr + pltpu.roll(kr, QK_ROPE, axis=1)
    kr_o[...] = kr[:, :QK_ROPE]
    krb_o[...] = jnp.where(lane < QK_ROPE, kr, 0.0).astype(BF16)

    fq_raw = proj(C_FQ, C_FK)
    for c in range(D_FOX // MXU_DIM):
        lo = c * MXU_DIM
        fq_o[:, lo:lo + MXU_DIM] = _seg_rms(fq_raw[:, lo:lo + MXU_DIM], b128_ref,
                                            gfq_ref[:, lo:lo + MXU_DIM]).astype(BF16)
    fk = _seg_rms(proj(C_FK, C_FV), b128_ref, gfk_ref[...])
    fk_o[...] = fk
    fkb_o[...] = fk.astype(BF16)
    fv = proj(C_FV, C_ZF)
    fv_o[...] = fv
    fvb_o[...] = fv.astype(BF16)
    fl = proj(C_FL, C_END) + bf_ref[...]
    lf = jnp.minimum(fl, 0.0) - jnp.log(1.0 + jnp.exp(-jnp.abs(fl)))
    lf_o[...] = lf[:, :H_FOX]

    sz_o[:, :D_MLA] = _silu(proj(C_ZM, C_FQ)).astype(BF16)
    sz_o[:, D_MLA:] = _silu(proj(C_ZF, C_FL)).astype(BF16)


def _project(x2d, shift, scale, per_row_mod, tm, consts, tabq, tabk):
    rows, d = x2d.shape
    nt = rows // tm
    if per_row_mod:
        mod_spec = pl.BlockSpec((tm, d), lambda r: (r, 0))
    else:
        tiles_per_b = nt // shift.shape[0]
        mod_spec = pl.BlockSpec((None, 1, d), lambda r: (r // tiles_per_b, 0, 0))
    tab_tiles = tabq.shape[0] // tm
    row = lambda w: pl.BlockSpec((tm, w), lambda r: (r, 0))
    out_widths = [(H_MLA * QK_PAD, BF16), (KV_LORA, F32), (QK_ROPE, F32), (D_MLA, BF16), (D_MLA, BF16),
                  (LANES, BF16), (D_FOX, BF16), (D_KV_FOX, F32), (D_KV_FOX, BF16), (D_KV_FOX, F32),
                  (D_KV_FOX, BF16), (H_FOX, F32), (D_MLA + D_FOX, BF16)]
    in_specs = ([row(d), mod_spec, mod_spec] + [_const_spec(c.shape) for c in consts]
                + [pl.BlockSpec((tm, QK_PAD - LANES), lambda r: (r % tab_tiles, 0)),
                   pl.BlockSpec((tm, LANES), lambda r: (r % tab_tiles, 0))])
    return pl.pallas_call(
        _proj_kernel,
        grid=(nt,),
        in_specs=in_specs,
        out_specs=[row(w) for w, _ in out_widths],
        out_shape=[jax.ShapeDtypeStruct((rows, w), dt) for w, dt in out_widths],
        compiler_params=_params("arbitrary"),
        name="proj",
    )(x2d, shift, scale, *consts, tabq, tabk)


def _split3(x):
    hi = x.astype(BF16)
    r1 = x - hi.astype(F32)
    mid = r1.astype(BF16)
    lo = (r1 - mid.astype(F32)).astype(BF16)
    return hi, mid, lo


def _dot3(x, m):
    hi, mid, lo = _split3(x)
    d = lambda a: jnp.dot(a, m, preferred_element_type=F32)
    return d(hi) + (d(mid) + d(lo))


def _cumsum_kernel(x_ref, tri_ref, o_ref):
    nblk = x_ref.shape[1] // LANES
    carry = jnp.zeros((x_ref.shape[0], 1), F32)
    for i in range(nblk):
        blk = _dot3(x_ref[:, i * LANES:(i + 1) * LANES], tri_ref[...]) + carry
        o_ref[:, i * LANES:(i + 1) * LANES] = blk
        carry = blk[:, LANES - 1:LANES]


def _cumsum_lanes(x_t):
    b, hh, s = x_t.shape
    tri = jnp.asarray(np.triu(np.ones((LANES, LANES), np.float32)), BF16)
    return pl.pallas_call(
        _cumsum_kernel,
        grid=(b,),
        in_specs=[pl.BlockSpec((None, hh, s), lambda i: (i, 0, 0)), _const_spec((LANES, LANES))],
        out_specs=pl.BlockSpec((None, hh, s), lambda i: (i, 0, 0)),
        out_shape=jax.ShapeDtypeStruct((b, hh, s), F32),
        compiler_params=_params("arbitrary"),
        name="cumsum",
    )(x_t, tri)


def _nt_dot(a, b):
    return lax.dot_general(a, b, (((1,), (1,)), ((), ())), preferred_element_type=F32)


def _p_attn_kernel(qcat_ref, kn_ref, krb_ref, v_ref, fq_ref, fkb_ref, fvb_ref, fcol_ref, frow_ref, sz_ref,
                   o_ref, m_m, l_m, acc_m, m_f, l_f, acc_f):
    qi = pl.program_id(1)
    ki = pl.program_id(2)
    tq = qcat_ref.shape[0]
    tk = kn_ref.shape[0]

    @pl.when(ki == 0)
    def _():
        m_m[...] = jnp.full_like(m_m, NEG_INF)
        m_f[...] = jnp.full_like(m_f, NEG_INF)
        l_m[...] = jnp.zeros_like(l_m)
        l_f[...] = jnp.zeros_like(l_f)
        acc_m[...] = jnp.zeros_like(acc_m)
        acc_f[...] = jnp.zeros_like(acc_f)

    def update(s, hd, m_ref, l_ref, acc_ref, vv):
        m_prev = m_ref[hd]
        m_new = jnp.maximum(m_prev, jnp.max(s, axis=-1, keepdims=True))
        alpha = jnp.exp(m_prev - m_new)
        p = jnp.exp(s - m_new)
        l_ref[hd] = alpha * l_ref[hd] + jnp.sum(p, axis=-1, keepdims=True)
        sl = slice(hd * V_DIM, (hd + 1) * V_DIM)
        acc_ref[:, sl] = alpha * acc_ref[:, sl] + jnp.dot(p.astype(BF16), vv, preferred_element_type=F32)
        m_ref[hd] = m_new

    def step(masked):
        if masked:
            keep = (lax.broadcasted_iota(jnp.int32, (tq, tk), 1)
                    <= lax.broadcasted_iota(jnp.int32, (tq, tk), 0))
        krb = krb_ref[...]
        for hd in range(H_MLA):
            kcat = jnp.concatenate([kn_ref[:, hd * QK_NOPE:(hd + 1) * QK_NOPE], krb], axis=1)
            s = _nt_dot(qcat_ref[:, hd * QK_PAD:(hd + 1) * QK_PAD], kcat)
            if masked:
                s = jnp.where(keep, s, NEG_INF)
            update(s, hd, m_m, l_m, acc_m, v_ref[:, hd * V_DIM:(hd + 1) * V_DIM])
        for hd in range(H_FOX):
            g = hd // FOX_GROUP
            s = _nt_dot(fq_ref[:, hd * FOX_HD:(hd + 1) * FOX_HD], fkb_ref[:, g * FOX_HD:(g + 1) * FOX_HD])
            s = s + (fcol_ref[:, hd:hd + 1] - frow_ref[hd:hd + 1, :])
            if masked:
                s = jnp.where(keep, s, NEG_INF)
            update(s, hd, m_f, l_f, acc_f, fvb_ref[:, g * FOX_HD:(g + 1) * FOX_HD])

    @pl.when(ki < qi)
    def _():
        step(False)

    @pl.when(ki == qi)
    def _():
        step(True)
        for hd in range(H_MLA):
            sl = slice(hd * V_DIM, (hd + 1) * V_DIM)
            o_ref[:, sl] = (acc_m[:, sl] / l_m[hd] * sz_ref[:, sl].astype(F32)).astype(BF16)
        for hd in range(H_FOX):
            sl = slice(hd * FOX_HD, (hd + 1) * FOX_HD)
            so = slice(D_MLA + hd * FOX_HD, D_MLA + (hd + 1) * FOX_HD)
            o_ref[:, so] = (acc_f[:, sl] / l_f[hd] * sz_ref[:, so].astype(F32)).astype(BF16)


def _prompt_attention(qcat, kn, krb, v, fq, fkb, fvb, fcol, frow, sz, tq):
    b, s, _ = qcat.shape
    nq = s // tq
    qmap = lambda bi, qi, ki: (bi, qi, 0)
    kmap = lambda bi, qi, ki: (bi, jnp.minimum(ki, qi), 0)
    qs = lambda w: pl.BlockSpec((None, tq, w), qmap)
    ks = lambda w: pl.BlockSpec((None, tq, w), kmap)
    return pl.pallas_call(
        _p_attn_kernel,
        grid=(b, nq, nq),
        in_specs=[qs(H_MLA * QK_PAD), ks(D_MLA), ks(LANES), ks(D_MLA), qs(D_FOX), ks(D_KV_FOX), ks(D_KV_FOX),
                  qs(H_FOX), pl.BlockSpec((None, H_FOX, tq), lambda bi, qi, ki: (bi, 0, jnp.minimum(ki, qi))),
                  qs(D_MLA + D_FOX)],
        out_specs=qs(D_MLA + D_FOX),
        out_shape=jax.ShapeDtypeStruct((b, s, D_MLA + D_FOX), BF16),
        scratch_shapes=[pltpu.VMEM((H_MLA, tq, 1), F32), pltpu.VMEM((H_MLA, tq, 1), F32),
                        pltpu.VMEM((tq, D_MLA), F32),
                        pltpu.VMEM((H_FOX, tq, 1), F32), pltpu.VMEM((H_FOX, tq, 1), F32),
                        pltpu.VMEM((tq, D_FOX), F32)],
        compiler_params=_params("arbitrary", "arbitrary", "arbitrary"),
        name="p_attn",
    )(qcat, kn, krb, v, fq, fkb, fvb, fcol, frow, sz)


def _out_kernel(x_ref, gate_ref, merged_ref, w_ref, o_ref):
    y = jnp.dot(merged_ref[...], w_ref[...], preferred_element_type=F32)
    o_ref[...] = x_ref[...] + gate_ref[...] * y


def _out_project(x2d, gate, merged, w_out_b, tm):
    rows, d = x2d.shape
    nt = rows // tm
    tiles_per_b = nt // gate.shape[0]
    return pl.pallas_call(
        _out_kernel,
        grid=(nt,),
        in_specs=[pl.BlockSpec((tm, d), lambda r: (r, 0)),
                  pl.BlockSpec((None, 1, d), lambda r: (r // tiles_per_b, 0, 0)),
                  pl.BlockSpec((tm, merged.shape[1]), lambda r: (r, 0)),
                  _const_spec(w_out_b.shape)],
        out_specs=pl.BlockSpec((tm, d), lambda r: (r, 0)),
        out_shape=jax.ShapeDtypeStruct((rows, d), F32),
        compiler_params=_params("arbitrary"),
        name="out_proj",
    )(x2d, gate, merged, w_out_b)


def _absorb_kernel(qcat_ref, gkn_ref, wukt_ref, o_ref):
    for hd in range(H_MLA):
        qn = qcat_ref[:, hd * QK_PAD:hd * QK_PAD + QK_NOPE].astype(F32) * gkn_ref[...]
        o_ref[hd] = jnp.dot(qn.astype(BF16), wukt_ref[hd * QK_NOPE:(hd + 1) * QK_NOPE, :],
                            preferred_element_type=F32).astype(BF16)


def _absorb(qcat_s, gkn, wukt):
    n = qcat_s.shape[0]
    return pl.pallas_call(
        _absorb_kernel,
        grid=(1,),
        in_specs=[_const_spec(qcat_s.shape), _const_spec(gkn.shape), _const_spec(wukt.shape)],
        out_specs=_const_spec((H_MLA, n, KV_LORA)),
        out_shape=jax.ShapeDtypeStruct((H_MLA, n, KV_LORA), BF16),
        compiler_params=_params("arbitrary"),
        name="absorb",
    )(qcat_s, gkn, wukt)


def _s_attn_kernel(pt_ref, qt_ref, qr_ref, fq_ref, lfn_ref, wukt_ref, sfx_ref, *refs, pages, page):
    ckv_p = refs[0 * pages:1 * pages]
    kr_p = refs[1 * pages:2 * pages]
    fk_p = refs[2 * pages:3 * pages]
    fv_p = refs[3 * pages:4 * pages]
    lf_p = refs[4 * pages:5 * pages]
    (mm_o, lm_o, am_o, mf_o, lf_o, af_o, cb, krb, fkb, fvb, lstack, carry) = refs[5 * pages:]
    c = pl.program_id(1)
    t = pages * page

    @pl.when(c == 0)
    def _():
        mm_o[...] = jnp.full_like(mm_o, NEG_INF)
        mf_o[...] = jnp.full_like(mf_o, NEG_INF)
        lm_o[...] = jnp.zeros_like(lm_o)
        lf_o[...] = jnp.zeros_like(lf_o)
        am_o[...] = jnp.zeros_like(am_o)
        af_o[...] = jnp.zeros_like(af_o)
        krb[...] = jnp.zeros_like(krb)
        carry[...] = lfn_ref[...]

    for j in range(pages):
        rows = slice(j * page, (j + 1) * page)
        cb[rows, :] = ckv_p[j][...].astype(BF16)
        krb[rows, 0:QK_ROPE] = kr_p[j][...].astype(BF16)
        for g in range(FOX_KV_HEADS):
            fkb[g, rows, :] = fk_p[j][pl.ds(g, page, stride=FOX_KV_HEADS), :].astype(BF16)
            fvb[g, rows, :] = fv_p[j][pl.ds(g, page, stride=FOX_KV_HEADS), :].astype(BF16)
        lstack[j * H_FOX:(j + 1) * H_FOX, :] = lf_p[j][...]

    cbv = cb[...]
    sub = min(t, 512)
    ssq = []
    for u in range(t // sub):
        kt = _nt_dot(wukt_ref[...], cb[u * sub:(u + 1) * sub, :])
        ssq.append(jnp.concatenate(
            [jnp.sum(kt[hd * QK_NOPE:(hd + 1) * QK_NOPE, :] ** 2, axis=0, keepdims=True) for hd in range(H_MLA)],
            axis=0))
    rstd = lax.rsqrt(jnp.concatenate(ssq, axis=1) * (1.0 / QK_NOPE) + EPS)
    s_m = _nt_dot(qt_ref[...], cbv) * rstd + _nt_dot(qr_ref[...], krb[...])

    def update(s, m_o, l_o):
        m_prev = m_o[:, 0:1]
        m_new = jnp.maximum(m_prev, jnp.max(s, axis=-1, keepdims=True))
        alpha = jnp.exp(m_prev - m_new)
        p = jnp.exp(s - m_new)
        l_new = alpha * l_o[:, 0:1] + jnp.sum(p, axis=-1, keepdims=True)
        m_o[...] = jnp.broadcast_to(m_new, m_o.shape)
        l_o[...] = jnp.broadcast_to(l_new, l_o.shape)
        return alpha, p.astype(BF16)

    alpha, p = update(s_m, mm_o, lm_o)
    am_o[...] = alpha * am_o[...] + jnp.dot(p, cbv, preferred_element_type=F32)

    sfx = _dot3(lstack[...], sfx_ref[...])
    run = carry[...]
    bias = [None] * pages
    for j in reversed(range(pages)):
        blk = sfx[j * H_FOX:(j + 1) * H_FOX, :]
        bias[j] = blk[:, :LANES] + run
        run = run + blk[:, LANES:]
    carry[...] = run
    fqv = fq_ref[...]
    s_f = jnp.concatenate(bias, axis=1)
    for g in range(FOX_KV_HEADS):
        s_f = s_f + _nt_dot(fqv[:, g * FOX_HD:(g + 1) * FOX_HD], fkb[g])
    alpha, p = update(s_f, mf_o, lf_o)
    pv = jnp.concatenate([jnp.dot(p, fvb[g], preferred_element_type=F32) for g in range(FOX_KV_HEADS)], axis=1)
    af_o[...] = alpha * af_o[...] + pv


def _sample_attention(layer, page_table, qt, qr, fqb, lfn, wukt, ckv_c, kr_c, fk_c, fv_c, lf_c, pages):
    n, n_pages = page_table.shape
    page = ckv_c.shape[2]
    assert page == LANES
    nc = n_pages // pages
    t = pages * page
    idx = np.arange(page)
    sfx = np.concatenate([(idx[:, None] > idx[None, :]).astype(np.float32),
                          np.ones((page, LANES), np.float32)], axis=1)
    sfx = jnp.asarray(sfx, BF16)

    def page_spec(shape, j):
        def imap(i, c, pt):
            return (layer, pt[i, (nc - 1 - c) * pages + j]) + (0,) * len(shape)
        return pl.BlockSpec((None, None) + shape, imap)

    seq = lambda shape: pl.BlockSpec((None,) + shape, lambda i, c, pt: (i,) + (0,) * len(shape))
    cst = lambda shape: pl.BlockSpec(shape, lambda i, c, pt: (0,) * len(shape))
    in_specs = [seq((H_MLA, KV_LORA)), seq((H_MLA, LANES)), seq((H_FOX, D_KV_FOX)), seq((H_FOX, LANES)),
                cst(wukt.shape), cst(sfx.shape)]
    operands = [qt, qr, fqb, lfn, wukt, sfx]
    for arr, shape in ((ckv_c, (page, KV_LORA)), (kr_c, (page, QK_ROPE)),
                       (fk_c, (page * FOX_KV_HEADS, FOX_HD)), (fv_c, (page * FOX_KV_HEADS, FOX_HD)),
                       (lf_c, (H_FOX, page))):
        for j in range(pages):
            in_specs.append(page_spec(shape, j))
            operands.append(arr)
    stat = jax.ShapeDtypeStruct((n, H_MLA, LANES), F32)
    accs = jax.ShapeDtypeStruct((n, H_MLA, KV_LORA), F32)
    grid_spec = pltpu.PrefetchScalarGridSpec(
        num_scalar_prefetch=1,
        grid=(n, nc),
        in_specs=in_specs,
        out_specs=[seq((H_MLA, LANES)), seq((H_MLA, LANES)), seq((H_MLA, KV_LORA)),
                   seq((H_FOX, LANES)), seq((H_FOX, LANES)), seq((H_FOX, D_KV_FOX))],
        scratch_shapes=[pltpu.VMEM((t, KV_LORA), BF16), pltpu.VMEM((t, LANES), BF16),
                        pltpu.VMEM((FOX_KV_HEADS, t, FOX_HD), BF16), pltpu.VMEM((FOX_KV_HEADS, t, FOX_HD), BF16),
                        pltpu.VMEM((pages * H_FOX, page), F32), pltpu.VMEM((H_FOX, LANES), F32)],
    )
    return pl.pallas_call(
        functools.partial(_s_attn_kernel, pages=pages, page=page),
        grid_spec=grid_spec,
        out_shape=[stat, stat, accs, stat, stat, accs],
        compiler_params=_params("arbitrary", "arbitrary"),
        name="s_attn",
    )(page_table, *operands)


def _finish_kernel(x_ref, gate_ref, qcat_ref, kn_ref, krb_ref, ckv_ref, fq_ref, fkb_ref, fvb_ref, sz_ref,
                   mm_ref, lm_ref, am_ref, mf_ref, lf_ref, af_ref, wuv_ref, wout_ref, o_ref, merged):
    krb = krb_ref[...].astype(F32)
    cself = ckv_ref[...].astype(BF16).astype(F32)

    def merge(s_self, m_past, l_past):
        m_tot = jnp.maximum(m_past, s_self)
        a = jnp.exp(m_past - m_tot)
        b = jnp.exp(s_self - m_tot)
        inv = 1.0 / (a * l_past + b)
        return a * inv, b * inv

    for hd in range(H_MLA):
        q = qcat_ref[:, hd * QK_PAD:(hd + 1) * QK_PAD].astype(F32)
        s_self = (jnp.sum(q[:, :QK_NOPE] * kn_ref[:, hd * QK_NOPE:(hd + 1) * QK_NOPE].astype(F32),
                          axis=-1, keepdims=True)
                  + jnp.sum(q[:, QK_NOPE:] * krb, axis=-1, keepdims=True))
        wa, wb = merge(s_self, mm_ref[hd], lm_ref[hd])
        lat = wa * am_ref[hd] + wb * cself
        o = jnp.dot(lat.astype(BF16), wuv_ref[:, hd * V_DIM:(hd + 1) * V_DIM], preferred_element_type=F32)
        sl = slice(hd * V_DIM, (hd + 1) * V_DIM)
        merged[:, sl] = (o * sz_ref[:, sl].astype(F32)).astype(BF16)
    for hd in range(H_FOX):
        g = hd // FOX_GROUP
        gs = slice(g * FOX_HD, (g + 1) * FOX_HD)
        s_self = jnp.sum(fq_ref[:, hd * FOX_HD:(hd + 1) * FOX_HD].astype(F32) * fkb_ref[:, gs].astype(F32),
                         axis=-1, keepdims=True)
        wa, wb = merge(s_self, mf_ref[hd], lf_ref[hd])
        o = wa * af_ref[hd][:, gs] + wb * fvb_ref[:, gs].astype(F32)
        so = slice(D_MLA + hd * FOX_HD, D_MLA + (hd + 1) * FOX_HD)
        merged[:, so] = (o * sz_ref[:, so].astype(F32)).astype(BF16)
    y = jnp.dot(merged[...], wout_ref[...], preferred_element_type=F32)
    o_ref[...] = x_ref[...] + gate_ref[...] * y


def _finish(*args):
    x = args[0]
    return pl.pallas_call(
        _finish_kernel,
        grid=(1,),
        in_specs=[_const_spec(a.shape) for a in args],
        out_specs=_const_spec(x.shape),
        out_shape=jax.ShapeDtypeStruct(x.shape, F32),
        scratch_shapes=[pltpu.VMEM((x.shape[0], D_MLA + D_FOX), BF16)],
        compiler_params=_params("arbitrary"),
        name="finish",
    )(*args)


def _swap_halves(a, axis=-1):
    lo, hi = jnp.split(a, 2, axis=axis)
    return jnp.concatenate([hi, lo], axis=axis)


def _seg_matrix(widths):
    n = sum(widths)
    m = np.zeros((n, n), np.float32)
    start = 0
    for w in widths:
        m[start:start + w, start:start + w] = 1.0 / w
        start += w
    return jnp.asarray(m, BF16)


def _rope_tables(pos):
    inv_freq = ROPE_THETA ** (-jnp.arange(0, QK_ROPE, 2, dtype=F32) / QK_ROPE)
    ang = pos.astype(F32)[:, None] * inv_freq[None, :]
    cos, sin = jnp.cos(ang), jnp.sin(ang)
    return jnp.concatenate([cos, cos, -sin, sin], axis=-1)


def _layer_consts(l, g_norm, w_in, b_f, g_qa, w_uq, g_qn_nope, g_qn_rope, g_kva, g_kn_nope, g_kn_rope,
                  w_uk, w_uv, g_fq, g_fk):
    w = w_in[l]
    d = w.shape[0]
    widths = (Q_LORA, KV_LORA, QK_ROPE, D_MLA, D_FOX, D_KV_FOX, D_KV_FOX, H_FOX, D_FOX)
    offs = np.concatenate([[0], np.cumsum(widths)])
    part = lambda i: w[:, offs[i]:offs[i + 1]]
    w_aug = jnp.concatenate([part(0), part(1), part(2), _swap_halves(part(2)), part(3), part(4), part(5),
                             part(6), part(8), part(7), jnp.zeros((d, LANES - H_FOX), F32)], axis=1).astype(BF16)
    uq = w_uq[l].reshape(Q_LORA, H_MLA, QK_HEAD)
    uq_rope = uq[:, :, QK_NOPE:]
    wuq = jnp.concatenate([uq[:, :, :QK_NOPE], uq_rope, _swap_halves(uq_rope)], axis=-1)
    wuq = wuq.reshape(Q_LORA, H_MLA * QK_PAD).astype(BF16)
    wuk = w_uk[l].reshape(KV_LORA, D_MLA)
    wuv = w_uv[l].reshape(KV_LORA, D_MLA)
    wukv = jnp.concatenate([wuk, wuv], axis=1).astype(BF16)
    gq = jnp.concatenate([g_qn_nope[l], g_qn_rope[l], _swap_halves(g_qn_rope[l])]) * MLA_SCALE
    row = lambda a: a.reshape(1, -1).astype(F32)
    consts = [row(g_norm[l]), w_aug, wuq, wukv, row(g_qa[l]), row(jnp.tile(gq, H_MLA)), row(g_kva[l]),
              row(jnp.tile(g_kn_nope[l], H_MLA)),
              row(jnp.concatenate([g_kn_rope[l], _swap_halves(g_kn_rope[l])])),
              row(jnp.tile(g_fq[l], H_FOX) * FOX_SCALE), row(jnp.tile(g_fk[l], FOX_KV_HEADS)),
              row(jnp.concatenate([b_f[l], jnp.zeros((LANES - H_FOX,), F32)])),
              _seg_matrix((QK_NOPE, QK_ROPE, QK_ROPE)), _seg_matrix((LANES, LANES)),
              _seg_matrix((QK_ROPE, QK_ROPE))]
    return consts, wuk.T.astype(BF16), wuv.astype(BF16), row(g_kn_nope[l])


def kernel(x_prompt, x_sample, c_prompt, c_sample, cache_mla_ckv, cache_mla_krope, cache_fox_k, cache_fox_v, cache_fox_logf, page_table, g_norm, w_ada, b_ada, w_in, b_f, g_qa, w_uq, g_qn_nope, g_qn_rope, g_kva, g_kn_nope, g_kn_rope, w_uk, w_uv, g_fq, g_fk, w_out):
    b, s, d = x_prompt.shape
    n, t_new, _ = x_sample.shape
    depth, n_pool, page = cache_mla_ckv.shape[:3]
    n_pages = page_table.shape[1]
    past = n_pages * page
    assert t_new == 1 and n % 8 == 0 and s % LANES == 0
    tm = min(256, s)
    tq = min(512, s)
    pages_per_step = min(16, n_pages)
    pad_rows = (-(b + n)) % 8

    tab_p = _rope_tables(jnp.arange(s))
    tab_s = jnp.broadcast_to(_rope_tables(past + jnp.arange(t_new)), (n, 2 * QK_ROPE))

    xp, xs = x_prompt, x_sample
    outs_p, outs_s = [], []
    for l in range(depth):
        consts, wukt, wuv_b, gkn_row = _layer_consts(l, g_norm, w_in, b_f, g_qa, w_uq, g_qn_nope, g_qn_rope,
                                                      g_kva, g_kn_nope, g_kn_rope, w_uk, w_uv, g_fq, g_fk)
        w_out_b = w_out[l].astype(BF16)

        c_all = jnp.concatenate([c_prompt, c_sample, jnp.zeros((pad_rows, d), F32)], axis=0)
        mod = _ada(c_all, w_ada[l], b_ada[l].reshape(1, -1))
        shift_p, scale_p, gate_p = (m.reshape(b, 1, d) for m in jnp.split(mod[:b], 3, axis=-1))
        shift_s, scale_s, gate_s = jnp.split(mod[b:b + n], 3, axis=-1)

        (qcat, ckv, kr, kn, v, krb, fq, fk, fkb, fv, fvb, lf, sz) = _project(
            xp.reshape(b * s, d), shift_p, scale_p, False, tm, consts, tab_p, tab_p)
        lf3 = lf.reshape(b, s, H_FOX)
        frow = _cumsum_lanes(jnp.swapaxes(lf3, 1, 2))
        fcol = jnp.swapaxes(frow, 1, 2)
        r3 = lambda a: a.reshape(b, s, a.shape[-1])
        merged = _prompt_attention(r3(qcat), r3(kn), r3(krb), r3(v), r3(fq), r3(fkb), r3(fvb), fcol, frow,
                                   r3(sz), tq)
        xp = _out_project(xp.reshape(b * s, d), gate_p, merged.reshape(b * s, -1), w_out_b, tq).reshape(b, s, d)
        outs_p.append((ckv.reshape(b, s, KV_LORA), kr.reshape(b, s, QK_ROPE),
                       fk.reshape(b, s, FOX_KV_HEADS, FOX_HD), fv.reshape(b, s, FOX_KV_HEADS, FOX_HD), lf3))

        (qcat_s, ckv_s, kr_s, kn_s, v_s, krb_s, fq_s, fk_s, fkb_s, fv_s, fvb_s, lf_s, sz_s) = _project(
            xs.reshape(n, d), shift_s, scale_s, True, n, consts, tab_s, tab_s)
        qt = jnp.swapaxes(_absorb(qcat_s, gkn_row, wukt), 0, 1)
        q3 = qcat_s.reshape(n, H_MLA, QK_PAD)
        qr = q3[:, :, QK_NOPE:]
        f3 = fq_s.reshape(n, FOX_KV_HEADS, FOX_GROUP, FOX_HD)
        eye = jnp.eye(FOX_KV_HEADS, dtype=BF16)
        fqb = jnp.einsum('ngjd,gk->ngjkd', f3, eye).reshape(n, H_FOX, D_KV_FOX)
        lfn = jnp.broadcast_to(lf_s[:, :, None], (n, H_FOX, LANES))
        fk_c = cache_fox_k.reshape(depth, n_pool, page * FOX_KV_HEADS, FOX_HD)
        fv_c = cache_fox_v.reshape(depth, n_pool, page * FOX_KV_HEADS, FOX_HD)
        lf_c = jnp.swapaxes(cache_fox_logf, 2, 3)
        stats = _sample_attention(l, page_table, qt, qr, fqb, lfn, wukt, cache_mla_ckv, cache_mla_krope,
                                  fk_c, fv_c, lf_c, pages_per_step)
        mm, lm, am, mf, lff, af = stats
        hn = lambda a: jnp.swapaxes(a, 0, 1)
        xs = _finish(xs.reshape(n, d), gate_s, qcat_s, kn_s, krb_s, ckv_s, fq_s, fkb_s, fvb_s, sz_s,
                     hn(mm)[:, :, :1], hn(lm)[:, :, :1], hn(am), hn(mf)[:, :, :1], hn(lff)[:, :, :1], hn(af),
                     wuv_b, w_out_b).reshape(n, t_new, d)
        outs_s.append((ckv_s.reshape(n, t_new, KV_LORA), kr_s.reshape(n, t_new, QK_ROPE),
                       fk_s.reshape(n, t_new, FOX_KV_HEADS, FOX_HD), fv_s.reshape(n, t_new, FOX_KV_HEADS, FOX_HD),
                       lf_s.reshape(n, t_new, H_FOX)))

    stack = lambda outs, i: jnp.stack([o[i] for o in outs])
    return (xp, xs, *(stack(outs_p, i) for i in range(5)), *(stack(outs_s, i) for i in range(5)))
```

```python
import functools

import jax
import jax.numpy as jnp
import numpy as np
from jax import lax
from jax.experimental import pallas as pl
from jax.experimental.pallas import tpu as pltpu

F32 = jnp.float32
BF16 = jnp.bfloat16

H_MLA = 8
QK_NOPE = 128
QK_ROPE = 64
QK_HEAD = QK_NOPE + QK_ROPE
QK_PAD = 256
V_DIM = 128
Q_LORA = 512
KV_LORA = 256
D_MLA = H_MLA * V_DIM
H_FOX = 8
FOX_KV_HEADS = 2
FOX_GROUP = H_FOX // FOX_KV_HEADS
FOX_HD = 128
D_FOX = H_FOX * FOX_HD
D_KV_FOX = FOX_KV_HEADS * FOX_HD
ROPE_THETA = 10000.0
EPS = 1e-6
MLA_SCALE = QK_HEAD ** -0.5
FOX_SCALE = FOX_HD ** -0.5
LANES = 128
MXU_DIM = 256
NEG_INF = -1e30
LOG2E = 1.4426950408889634

C_QA = 0
C_KVA = C_QA + Q_LORA
C_KR = C_KVA + KV_LORA
C_ZM = C_KR + 2 * QK_ROPE
C_FQ = C_ZM + D_MLA
C_FK = C_FQ + D_FOX
C_FV = C_FK + D_KV_FOX
C_ZF = C_FV + D_KV_FOX
C_FL = C_ZF + D_FOX
C_END = C_FL + LANES

VMEM_LIMIT = 56 * 1024 * 1024


def _params(*sem):
    return pltpu.CompilerParams(dimension_semantics=sem, vmem_limit_bytes=VMEM_LIMIT)


def _const_spec(shape):
    nd = len(shape)
    return pl.BlockSpec(shape, lambda *_: (0,) * nd, pipeline_mode=pl.Buffered(1))


def _ada_kernel(c_ref, w_ref, b_ref, o_ref):
    c = c_ref[...]
    a = (c / (1.0 + jnp.exp(-c))).astype(BF16)
    o_ref[...] = jnp.dot(a, w_ref[...].astype(BF16), preferred_element_type=F32) + b_ref[...]


def _ada(c_all, w_ada, b_ada):
    rows, d = c_all.shape
    n = w_ada.shape[1]
    tn = 512
    return pl.pallas_call(
        _ada_kernel,
        grid=(n // tn,),
        in_specs=[pl.BlockSpec((rows, d), lambda j: (0, 0)),
                  pl.BlockSpec((d, tn), lambda j: (0, j)),
                  pl.BlockSpec((1, tn), lambda j: (0, j))],
        out_specs=pl.BlockSpec((rows, tn), lambda j: (0, j)),
        out_shape=jax.ShapeDtypeStruct((rows, n), F32),
        compiler_params=_params("arbitrary"),
        name="ada",
    )(c_all, w_ada, b_ada)


def _seg_rms(x, b_ref, gain):
    ms = jnp.dot((x * x).astype(BF16), b_ref[...], preferred_element_type=F32)
    return x * lax.rsqrt(ms + EPS) * gain


def _silu(z):
    return z / (1.0 + jnp.exp(-z))


def _proj_kernel(x_ref, shift_ref, scale_ref, gnorm_ref, win_ref, wuq_ref, wukv_ref,
                 gqa_ref, gq_ref, gkva_ref, gkn_ref, gkr_ref, gfq_ref, gfk_ref, bf_ref,
                 bq_ref, b128_ref, bkr_ref, tabq_ref, tabk_ref,
                 qcat_o, ckv_o, kr_o, kn_o, v_o, krb_o, fq_o, fk_o, fkb_o, fv_o, fvb_o, lf_o, sz_o):
    x = x_ref[...]
    ms = jnp.mean(x * x, axis=-1, keepdims=True)
    xn = x * lax.rsqrt(ms + EPS) * gnorm_ref[...]
    h = xn * (1.0 + scale_ref[...]) + shift_ref[...]
    hb = h.astype(BF16)

    def proj(lo, hi):
        return jnp.dot(hb, win_ref[:, lo:hi], preferred_element_type=F32)

    lane = lax.broadcasted_iota(jnp.int32, (x.shape[0], LANES), 1)

    qa = proj(C_QA, C_KVA)
    qn = qa * lax.rsqrt(jnp.mean(qa * qa, axis=-1, keepdims=True) + EPS) * gqa_ref[...]
    q_raw = jnp.dot(qn.astype(BF16), wuq_ref[...], preferred_element_type=F32)
    tabq = tabq_ref[...]
    for hd in range(H_MLA):
        lo = hd * QK_PAD
        y = _seg_rms(q_raw[:, lo:lo + QK_PAD], bq_ref, gq_ref[:, lo:lo + QK_PAD])
        t = y[:, LANES:] * tabq
        rot = t + pltpu.roll(t, QK_ROPE, axis=1)
        qcat_o[:, lo:lo + LANES] = y[:, :LANES].astype(BF16)
        qcat_o[:, lo + LANES:lo + QK_PAD] = jnp.where(lane < QK_ROPE, rot, 0.0).astype(BF16)

    kva = proj(C_KVA, C_KR)
    ckv = kva * lax.rsqrt(jnp.mean(kva * kva, axis=-1, keepdims=True) + EPS) * gkva_ref[...]
    ckv_o[...] = ckv
    kv_raw = jnp.dot(ckv.astype(BF16), wukv_ref[...], preferred_element_type=F32)
    for c in range(D_MLA // MXU_DIM):
        lo = c * MXU_DIM
        kn_o[:, lo:lo + MXU_DIM] = _seg_rms(kv_raw[:, lo:lo + MXU_DIM], b128_ref,
                                            gkn_ref[:, lo:lo + MXU_DIM]).astype(BF16)
    v_o[...] = kv_raw[:, D_MLA:].astype(BF16)

    kr = _seg_rms(proj(C_KR, C_ZM), bkr_ref, gkr_ref[...]) * tabk_ref[...]
    kr = kr + pltpu.roll(kr, QK_ROPE, axis=1)
    kr_o[...] = kr[:, :QK_ROPE]
    krb_o[...] = jnp.where(lane < QK_ROPE, kr, 0.0).astype(BF16)

    fq_raw = proj(C_FQ, C_FK)
    for c in range(D_FOX // MXU_DIM):
        lo = c * MXU_DIM
        fq_o[:, lo:lo + MXU_DIM] = _seg_rms(fq_raw[:, lo:lo + MXU_DIM], b128_ref,
                                            gfq_ref[:, lo:lo + MXU_DIM]).astype(BF16)
    fk = _seg_rms(proj(C_FK, C_FV), b128_ref, gfk_ref[...])
    fk_o[...] = fk
    fkb_o[...] = fk.astype(BF16)
    fv = proj(C_FV, C_ZF)
    fv_o[...] = fv
    fvb_o[...] = fv.astype(BF16)
    fl = proj(C_FL, C_END) + bf_ref[...]
    lf = jnp.minimum(fl, 0.0) - jnp.log(1.0 + jnp.exp(-jnp.abs(fl)))
    lf_o[...] = lf[:, :H_FOX]

    sz_o[:, :D_MLA] = _silu(proj(C_ZM, C_FQ)).astype(BF16)
    sz_o[:, D_MLA:] = _silu(proj(C_ZF, C_FL)).astype(BF16)


def _project(x2d, shift, scale, per_row_mod, tm, consts, tabq, tabk):
    rows, d = x2d.shape
    nt = rows // tm
    if per_row_mod:
        mod_spec = pl.BlockSpec((tm, d), lambda r: (r, 0))
    else:
        tiles_per_b = nt // shift.shape[0]
        mod_spec = pl.BlockSpec((None, 1, d), lambda r: (r // tiles_per_b, 0, 0))
    tab_tiles = tabq.shape[0] // tm
    row = lambda w: pl.BlockSpec((tm, w), lambda r: (r, 0))
    out_widths = [(H_MLA * QK_PAD, BF16), (KV_LORA, F32), (QK_ROPE, F32), (D_MLA, BF16), (D_MLA, BF16),
                  (LANES, BF16), (D_FOX, BF16), (D_KV_FOX, F32), (D_KV_FOX, BF16), (D_KV_FOX, F32),
                  (D_KV_FOX, BF16), (H_FOX, F32), (D_MLA + D_FOX, BF16)]
    in_specs = ([row(d), mod_spec, mod_spec] + [_const_spec(c.shape) for c in consts]
                + [pl.BlockSpec((tm, QK_PAD - LANES), lambda r: (r % tab_tiles, 0)),
                   pl.BlockSpec((tm, LANES), lambda r: (r % tab_tiles, 0))])
    return pl.pallas_call(
        _proj_kernel,
        grid=(nt,),
        in_specs=in_specs,
        out_specs=[row(w) for w, _ in out_widths],
        out_shape=[jax.ShapeDtypeStruct((rows, w), dt) for w, dt in out_widths],
        compiler_params=_params("arbitrary"),
        name="proj",
    )(x2d, shift, scale, *consts, tabq, tabk)


def _split3(x):
    hi = x.astype(BF16)
    r1 = x - hi.astype(F32)
    mid = r1.astype(BF16)
    lo = (r1 - mid.astype(F32)).astype(BF16)
    return hi, mid, lo


def _dot3(x, m):
    hi, mid, lo = _split3(x)
    d = lambda a: jnp.dot(a, m, preferred_element_type=F32)
    return d(hi) + (d(mid) + d(lo))


def _cumsum_kernel(x_ref, tri_ref, hi_ref, mid_ref, lo_ref):
    nblk = x_ref.shape[1] // LANES
    carry = jnp.zeros((x_ref.shape[0], 1), F32)
    for i in range(nblk):
        sl = slice(i * LANES, (i + 1) * LANES)
        blk = _dot3(x_ref[:, sl], tri_ref[...]) + carry
        carry = blk[:, LANES - 1:LANES]
        hi_ref[:, sl], mid_ref[:, sl], lo_ref[:, sl] = _split3(blk * LOG2E)


def _cumsum_lanes(x_t):
    b, hh, s = x_t.shape
    tri = jnp.asarray(np.triu(np.ones((LANES, LANES), np.float32)), BF16)
    spec = pl.BlockSpec((None, hh, s), lambda i: (i, 0, 0))
    return pl.pallas_call(
        _cumsum_kernel,
        grid=(b,),
        in_specs=[spec, _const_spec((LANES, LANES))],
        out_specs=[spec] * 3,
        out_shape=[jax.ShapeDtypeStruct((b, hh, s), BF16)] * 3,
        compiler_params=_params("arbitrary"),
        name="cumsum",
    )(x_t, tri)


def _fox_bias_lanes(f_terms):
    b, hh, s, nt = f_terms.shape
    f = jnp.transpose(f_terms, (0, 2, 1, 3))
    slot = jnp.arange(hh) % FOX_GROUP
    sel = (jnp.arange(FOX_GROUP)[None, :] == slot[:, None]).astype(BF16)
    sel = jnp.broadcast_to(jnp.repeat(sel, nt, axis=1)[None, None], (b, s, hh, FOX_GROUP * nt))
    pad_q = jnp.zeros((b, s, hh, LANES - nt - FOX_GROUP * nt), BF16)
    qaux = jnp.concatenate([f, sel, pad_q], axis=-1).reshape(b, s, hh * LANES)
    neg = (-f).reshape(b, s, FOX_KV_HEADS, FOX_GROUP * nt)
    ones = jnp.ones((b, s, FOX_KV_HEADS, nt), BF16)
    pad_k = jnp.zeros((b, s, FOX_KV_HEADS, LANES - nt - FOX_GROUP * nt), BF16)
    kaux = jnp.concatenate([ones, neg, pad_k], axis=-1).reshape(b, s, FOX_KV_HEADS * LANES)
    return qaux, kaux


def _nt_dot(a, b):
    return lax.dot_general(a, b, (((1,), (1,)), ((), ())), preferred_element_type=F32)


def _p_attn_kernel(qcat_ref, kn_ref, krb_ref, v_ref, fq_ref, qaux_ref, fkb_ref, kaux_ref, fvb_ref, sz_ref,
                   o_ref, m_m, acc_m, m_f, acc_f):
    qi = pl.program_id(1)
    ki = pl.program_id(2)
    tq = qcat_ref.shape[0]
    tk = kn_ref.shape[0]
    acc_w = V_DIM + LANES

    @pl.when(ki == 0)
    def _():
        m_m[...] = jnp.full_like(m_m, NEG_INF)
        m_f[...] = jnp.full_like(m_f, NEG_INF)
        acc_m[...] = jnp.zeros_like(acc_m)
        acc_f[...] = jnp.zeros_like(acc_f)

    def update(s, hd, m_ref, acc_ref, vv, ones):
        m_prev = m_ref[hd]
        m_new = jnp.maximum(m_prev, jnp.max(s, axis=-1, keepdims=True))
        alpha = jnp.exp2(m_prev - m_new)
        p = jnp.exp2(s - jnp.concatenate([m_new] * (tk // LANES), axis=1)).astype(BF16)
        pv = jnp.dot(p, jnp.concatenate([vv, ones], axis=1), preferred_element_type=F32)
        sl = slice(hd * acc_w, (hd + 1) * acc_w)
        acc_ref[:, sl] = jnp.concatenate([alpha] * (acc_w // LANES), axis=1) * acc_ref[:, sl] + pv
        m_ref[hd] = m_new

    def step(masked):
        if masked:
            keep = (lax.broadcasted_iota(jnp.int32, (tq, tk), 1)
                    <= lax.broadcasted_iota(jnp.int32, (tq, tk), 0))
        ones = jnp.ones((tk, LANES), BF16)
        krb = krb_ref[...]
        for hd in range(H_MLA):
            kcat = jnp.concatenate([kn_ref[:, hd * QK_NOPE:(hd + 1) * QK_NOPE], krb], axis=1)
            s = _nt_dot(qcat_ref[:, hd * QK_PAD:(hd + 1) * QK_PAD], kcat)
            if masked:
                s = jnp.where(keep, s, NEG_INF)
            update(s, hd, m_m, acc_m, v_ref[:, hd * V_DIM:(hd + 1) * V_DIM], ones)
        for hd in range(H_FOX):
            g = hd // FOX_GROUP
            hs = slice(hd * FOX_HD, (hd + 1) * FOX_HD)
            gs = slice(g * FOX_HD, (g + 1) * FOX_HD)
            s = _nt_dot(jnp.concatenate([fq_ref[:, hs], qaux_ref[:, hs]], axis=1),
                        jnp.concatenate([fkb_ref[:, gs], kaux_ref[:, gs]], axis=1))
            if masked:
                s = jnp.where(keep, s, NEG_INF)
            update(s, hd, m_f, acc_f, fvb_ref[:, gs], ones)

    @pl.when(ki < qi)
    def _():
        step(False)

    @pl.when(ki == qi)
    def _():
        step(True)
        for acc_ref, base in ((acc_m, 0), (acc_f, D_MLA)):
            for hd in range(H_MLA):
                lo = hd * acc_w
                so = slice(base + hd * V_DIM, base + (hd + 1) * V_DIM)
                o = acc_ref[:, lo:lo + V_DIM] / acc_ref[:, lo + V_DIM:lo + acc_w]
                o_ref[:, so] = (o * sz_ref[:, so].astype(F32)).astype(BF16)


def _prompt_attention(qcat, kn, krb, v, fq, qaux, fkb, kaux, fvb, sz, tq):
    b, s, _ = qcat.shape
    nq = s // tq
    qmap = lambda bi, qi, ki: (bi, qi, 0)
    kmap = lambda bi, qi, ki: (bi, jnp.minimum(ki, qi), 0)
    qs = lambda w: pl.BlockSpec((None, tq, w), qmap)
    ks = lambda w: pl.BlockSpec((None, tq, w), kmap)
    acc_w = V_DIM + LANES
    return pl.pallas_call(
        _p_attn_kernel,
        grid=(b, nq, nq),
        in_specs=[qs(H_MLA * QK_PAD), ks(D_MLA), ks(LANES), ks(D_MLA), qs(D_FOX), qs(H_FOX * LANES),
                  ks(D_KV_FOX), ks(FOX_KV_HEADS * LANES), ks(D_KV_FOX), qs(D_MLA + D_FOX)],
        out_specs=qs(D_MLA + D_FOX),
        out_shape=jax.ShapeDtypeStruct((b, s, D_MLA + D_FOX), BF16),
        scratch_shapes=[pltpu.VMEM((H_MLA, tq, LANES), F32), pltpu.VMEM((tq, H_MLA * acc_w), F32),
                        pltpu.VMEM((H_FOX, tq, LANES), F32), pltpu.VMEM((tq, H_FOX * acc_w), F32)],
        compiler_params=_params("arbitrary", "arbitrary", "arbitrary"),
        name="p_attn",
    )(qcat, kn, krb, v, fq, qaux, fkb, kaux, fvb, sz)


def _out_kernel(x_ref, gate_ref, merged_ref, w_ref, o_ref):
    y = jnp.dot(merged_ref[...], w_ref[...], preferred_element_type=F32)
    o_ref[...] = x_ref[...] + gate_ref[...] * y


def _out_project(x2d, gate, merged, w_out_b, tm):
    rows, d = x2d.shape
    nt = rows // tm
    tiles_per_b = nt // gate.shape[0]
    return pl.pallas_call(
        _out_kernel,
        grid=(nt,),
        in_specs=[pl.BlockSpec((tm, d), lambda r: (r, 0)),
                  pl.BlockSpec((None, 1, d), lambda r: (r // tiles_per_b, 0, 0)),
                  pl.BlockSpec((tm, merged.shape[1]), lambda r: (r, 0)),
                  _const_spec(w_out_b.shape)],
        out_specs=pl.BlockSpec((tm, d), lambda r: (r, 0)),
        out_shape=jax.ShapeDtypeStruct((rows, d), F32),
        compiler_params=_params("arbitrary"),
        name="out_proj",
    )(x2d, gate, merged, w_out_b)


def _absorb_kernel(qcat_ref, gkn_ref, wukt_ref, o_ref):
    for hd in range(H_MLA):
        qn = qcat_ref[:, hd * QK_PAD:hd * QK_PAD + QK_NOPE].astype(F32) * gkn_ref[...]
        o_ref[hd] = jnp.dot(qn.astype(BF16), wukt_ref[hd * QK_NOPE:(hd + 1) * QK_NOPE, :],
                            preferred_element_type=F32).astype(BF16)


def _absorb(qcat_s, gkn, wukt):
    n = qcat_s.shape[0]
    return pl.pallas_call(
        _absorb_kernel,
        grid=(1,),
        in_specs=[_const_spec(qcat_s.shape), _const_spec(gkn.shape), _const_spec(wukt.shape)],
        out_specs=_const_spec((H_MLA, n, KV_LORA)),
        out_shape=jax.ShapeDtypeStruct((H_MLA, n, KV_LORA), BF16),
        compiler_params=_params("arbitrary"),
        name="absorb",
    )(qcat_s, gkn, wukt)


def _s_attn_kernel(pt_ref, qt_ref, qr_ref, fq_ref, lfn_ref, wukt_ref, sfx_ref,
                   ckv_hbm, kr_hbm, fk_hbm, fv_hbm, lf_hbm,
                   mm_o, lm_o, am_o, mf_o, lf_o, af_o,
                   cbuf, kbuf, fkbuf, fvbuf, lbuf, sems, cb, krb, fkb, fvb, lstack, carry,
                   *, layer, pages, page, nc, n):
    i = pl.program_id(0)
    c = pl.program_id(1)
    g = i * nc + c
    slot = lax.rem(g, 2)
    t = pages * page
    rows_kv = page * FOX_KV_HEADS

    def page_copies(seq, chunk, slot_):
        cps = []
        for j in range(pages):
            pid = pt_ref[seq, (nc - 1 - chunk) * pages + j]
            rows = pl.ds(j * page, page)
            rows2 = pl.ds(j * rows_kv, rows_kv)
            for k, (src, dst) in enumerate(((ckv_hbm, cbuf.at[slot_, rows]), (kr_hbm, kbuf.at[slot_, rows]),
                                            (fk_hbm, fkbuf.at[slot_, rows2]), (fv_hbm, fvbuf.at[slot_, rows2]),
                                            (lf_hbm, lbuf.at[slot_, rows]))):
                cps.append(pltpu.make_async_copy(src.at[layer, pid], dst, sems.at[k, slot_]))
        return cps

    @pl.when(g == 0)
    def _():
        krb[...] = jnp.zeros_like(krb)
        for cp in page_copies(0, 0, 0):
            cp.start()

    @pl.when(c == 0)
    def _():
        mm_o[...] = jnp.full_like(mm_o, NEG_INF)
        mf_o[...] = jnp.full_like(mf_o, NEG_INF)
        lm_o[...] = jnp.zeros_like(lm_o)
        lf_o[...] = jnp.zeros_like(lf_o)
        am_o[...] = jnp.zeros_like(am_o)
        af_o[...] = jnp.zeros_like(af_o)
        carry[...] = lfn_ref[...] * LOG2E

    for cp in page_copies(i, c, slot):
        cp.wait()
    last_c = c == nc - 1
    c_next = jnp.where(last_c, 0, c + 1)
    i_next = jnp.where(last_c, jnp.where(i == n - 1, 0, i + 1), i)
    nxt = page_copies(i_next, c_next, 1 - slot)
    per_page = len(nxt) // pages
    for j in range(pages):
        for cp in nxt[j * per_page:(j + 1) * per_page]:
            cp.start()
        rows = slice(j * page, (j + 1) * page)
        cb[rows, :] = cbuf[slot, rows, :].astype(BF16)
        krb[rows, 0:QK_ROPE] = kbuf[slot, rows, :].astype(BF16)
        for gg in range(FOX_KV_HEADS):
            src_rows = pl.ds(j * rows_kv + gg, page, stride=FOX_KV_HEADS)
            fkb[gg, rows, :] = fkbuf[slot, src_rows, :].astype(BF16)
            fvb[gg, rows, :] = fvbuf[slot, src_rows, :].astype(BF16)
    for j in range(pages):
        lstack[j * H_FOX:(j + 1) * H_FOX, :] = lbuf[slot, j * page:(j + 1) * page, :].T

    def update(s, m_o, l_o):
        m_prev = m_o[...]
        m_new = jnp.maximum(m_prev, jnp.max(s, axis=-1, keepdims=True))
        alpha = jnp.exp2(m_prev - m_new)
        p = jnp.exp2(s - jnp.concatenate([m_new] * (t // LANES), axis=1))
        l_o[...] = alpha * l_o[...] + jnp.sum(p, axis=-1, keepdims=True)
        m_o[...] = m_new
        return jnp.concatenate([alpha, alpha], axis=1), p.astype(BF16)

    lhs = jnp.concatenate([wukt_ref[...], qt_ref[...]], axis=0)
    sub = min(t, 512)
    n_sub = t // sub

    def mla_logits(u):
        kt = _nt_dot(lhs, cb[u * sub:(u + 1) * sub, :])
        ssq = jnp.concatenate(
            [jnp.sum(kt[hd * QK_NOPE:(hd + 1) * QK_NOPE, :] ** 2, axis=0, keepdims=True) for hd in range(H_MLA)],
            axis=0)
        return kt[H_MLA * QK_NOPE:H_MLA * QK_NOPE + H_MLA, :] * lax.rsqrt(ssq * (1.0 / QK_NOPE) + EPS)

    def fox_logits():
        sfx = _dot3(lstack[...] * LOG2E, sfx_ref[...])
        run = carry[...]
        bias = [None] * pages
        for j in reversed(range(pages)):
            blk = sfx[j * H_FOX:(j + 1) * H_FOX, :]
            bias[j] = blk[:, :LANES] + run
            run = run + blk[:, LANES:]
        carry[...] = run
        fqv = fq_ref[...]
        s_f = jnp.concatenate(bias, axis=1)
        for gg in range(FOX_KV_HEADS):
            s_f = s_f + _nt_dot(fqv[:, gg * FOX_HD:(gg + 1) * FOX_HD], fkb[gg])
        return s_f

    def fox_values(alpha, p):
        pv = jnp.concatenate([jnp.dot(p, fvb[gg], preferred_element_type=F32) for gg in range(FOX_KV_HEADS)],
                             axis=1)
        af_o[...] = alpha * af_o[...] + pv

    s_m = []
    fox_p = None
    logits_at, values_at = min(1, n_sub - 1), min(2, n_sub - 1)
    for u in range(n_sub):
        s_m.append(mla_logits(u))
        if u == logits_at:
            fox_p = update(fox_logits(), mf_o, lf_o)
        if u == values_at:
            fox_values(*fox_p)
    s_m = jnp.concatenate(s_m, axis=1) + _nt_dot(qr_ref[...], krb[...])
    alpha, p = update(s_m, mm_o, lm_o)
    am_o[...] = alpha * am_o[...] + jnp.dot(p, cb[...], preferred_element_type=F32)

    @pl.when(g == n * nc - 1)
    def _():
        for cp in page_copies(0, 0, 1 - slot):
            cp.wait()


def _sample_attention(layer, page_table, qt, qr, fqb, lfn, wukt, ckv_c, kr_c, fk_c, fv_c, lf_c, pages):
    n, n_pages = page_table.shape
    page = ckv_c.shape[2]
    assert page == LANES
    nc = n_pages // pages
    t = pages * page
    idx = np.arange(page)
    sfx = np.concatenate([(idx[:, None] > idx[None, :]).astype(np.float32),
                          np.ones((page, LANES), np.float32)], axis=1)
    sfx = jnp.asarray(sfx, BF16)

    seq = lambda shape: pl.BlockSpec((None,) + shape, lambda i, c, pt: (i,) + (0,) * len(shape))
    cst = lambda shape: pl.BlockSpec(shape, lambda i, c, pt: (0,) * len(shape))
    hbm = pl.BlockSpec(memory_space=pl.ANY)
    in_specs = [seq((2 * H_MLA, KV_LORA)), seq((H_MLA, LANES)), seq((H_FOX, D_KV_FOX)), seq((H_FOX, LANES)),
                cst(wukt.shape), cst(sfx.shape), hbm, hbm, hbm, hbm, hbm]
    stat = jax.ShapeDtypeStruct((n, H_MLA, LANES), F32)
    accs = jax.ShapeDtypeStruct((n, H_MLA, KV_LORA), F32)
    n_arrays = 5
    grid_spec = pltpu.PrefetchScalarGridSpec(
        num_scalar_prefetch=1,
        grid=(n, nc),
        in_specs=in_specs,
        out_specs=[seq((H_MLA, LANES)), seq((H_MLA, LANES)), seq((H_MLA, KV_LORA)),
                   seq((H_FOX, LANES)), seq((H_FOX, LANES)), seq((H_FOX, D_KV_FOX))],
        scratch_shapes=[pltpu.VMEM((2, t, KV_LORA), F32), pltpu.VMEM((2, t, QK_ROPE), F32),
                        pltpu.VMEM((2, t * FOX_KV_HEADS, FOX_HD), F32),
                        pltpu.VMEM((2, t * FOX_KV_HEADS, FOX_HD), F32),
                        pltpu.VMEM((2, t, H_FOX), F32), pltpu.SemaphoreType.DMA((n_arrays, 2)),
                        pltpu.VMEM((t, KV_LORA), BF16), pltpu.VMEM((t, LANES), BF16),
                        pltpu.VMEM((FOX_KV_HEADS, t, FOX_HD), BF16), pltpu.VMEM((FOX_KV_HEADS, t, FOX_HD), BF16),
                        pltpu.VMEM((pages * H_FOX, page), F32), pltpu.VMEM((H_FOX, LANES), F32)],
    )
    return pl.pallas_call(
        functools.partial(_s_attn_kernel, layer=layer, pages=pages, page=page, nc=nc, n=n),
        grid_spec=grid_spec,
        out_shape=[stat, stat, accs, stat, stat, accs],
        compiler_params=_params("arbitrary", "arbitrary"),
        name="s_attn",
    )(page_table, qt, qr, fqb, lfn, wukt, sfx, ckv_c, kr_c, fk_c, fv_c, lf_c)


def _finish_kernel(x_ref, gate_ref, qcat_ref, kn_ref, krb_ref, ckv_ref, fq_ref, fkb_ref, fvb_ref, sz_ref,
                   mm_ref, lm_ref, am_ref, mf_ref, lf_ref, af_ref, wuv_ref, wout_ref, o_ref, merged):
    krb = krb_ref[...].astype(F32)
    cself = ckv_ref[...].astype(BF16).astype(F32)

    def merge(s_self, m_past, l_past):
        m_tot = jnp.maximum(m_past, s_self)
        a = jnp.exp2(m_past - m_tot)
        b = jnp.exp2(s_self - m_tot)
        inv = 1.0 / (a * l_past + b)
        return a * inv, b * inv

    for hd in range(H_MLA):
        q = qcat_ref[:, hd * QK_PAD:(hd + 1) * QK_PAD].astype(F32)
        s_self = (jnp.sum(q[:, :QK_NOPE] * kn_ref[:, hd * QK_NOPE:(hd + 1) * QK_NOPE].astype(F32),
                          axis=-1, keepdims=True)
                  + jnp.sum(q[:, QK_NOPE:] * krb, axis=-1, keepdims=True))
        wa, wb = merge(s_self, mm_ref[hd], lm_ref[hd])
        lat = wa * am_ref[hd] + wb * cself
        o = jnp.dot(lat.astype(BF16), wuv_ref[:, hd * V_DIM:(hd + 1) * V_DIM], preferred_element_type=F32)
        sl = slice(hd * V_DIM, (hd + 1) * V_DIM)
        merged[:, sl] = (o * sz_ref[:, sl].astype(F32)).astype(BF16)
    for hd in range(H_FOX):
        g = hd // FOX_GROUP
        gs = slice(g * FOX_HD, (g + 1) * FOX_HD)
        s_self = jnp.sum(fq_ref[:, hd * FOX_HD:(hd + 1) * FOX_HD].astype(F32) * fkb_ref[:, gs].astype(F32),
                         axis=-1, keepdims=True)
        wa, wb = merge(s_self, mf_ref[hd], lf_ref[hd])
        o = wa * af_ref[hd][:, gs] + wb * fvb_ref[:, gs].astype(F32)
        so = slice(D_MLA + hd * FOX_HD, D_MLA + (hd + 1) * FOX_HD)
        merged[:, so] = (o * sz_ref[:, so].astype(F32)).astype(BF16)
    y = jnp.dot(merged[...], wout_ref[...], preferred_element_type=F32)
    o_ref[...] = x_ref[...] + gate_ref[...] * y


def _finish(*args):
    x = args[0]
    return pl.pallas_call(
        _finish_kernel,
        grid=(1,),
        in_specs=[_const_spec(a.shape) for a in args],
        out_specs=_const_spec(x.shape),
        out_shape=jax.ShapeDtypeStruct(x.shape, F32),
        scratch_shapes=[pltpu.VMEM((x.shape[0], D_MLA + D_FOX), BF16)],
        compiler_params=_params("arbitrary"),
        name="finish",
    )(*args)


def _swap_halves(a, axis=-1):
    lo, hi = jnp.split(a, 2, axis=axis)
    return jnp.concatenate([hi, lo], axis=axis)


def _seg_matrix(widths):
    n = sum(widths)
    m = np.zeros((n, n), np.float32)
    start = 0
    for w in widths:
        m[start:start + w, start:start + w] = 1.0 / w
        start += w
    return jnp.asarray(m, BF16)


def _rope_tables(pos):
    inv_freq = ROPE_THETA ** (-jnp.arange(0, QK_ROPE, 2, dtype=F32) / QK_ROPE)
    ang = pos.astype(F32)[:, None] * inv_freq[None, :]
    cos, sin = jnp.cos(ang), jnp.sin(ang)
    return jnp.concatenate([cos, cos, -sin, sin], axis=-1)


def _layer_consts(l, g_norm, w_in, b_f, g_qa, w_uq, g_qn_nope, g_qn_rope, g_kva, g_kn_nope, g_kn_rope,
                  w_uk, w_uv, g_fq, g_fk):
    w = w_in[l]
    d = w.shape[0]
    widths = (Q_LORA, KV_LORA, QK_ROPE, D_MLA, D_FOX, D_KV_FOX, D_KV_FOX, H_FOX, D_FOX)
    offs = np.concatenate([[0], np.cumsum(widths)])
    part = lambda i: w[:, offs[i]:offs[i + 1]]
    w_aug = jnp.concatenate([part(0), part(1), part(2), _swap_halves(part(2)), part(3), part(4), part(5),
                             part(6), part(8), part(7), jnp.zeros((d, LANES - H_FOX), F32)], axis=1).astype(BF16)
    uq = w_uq[l].reshape(Q_LORA, H_MLA, QK_HEAD)
    uq_rope = uq[:, :, QK_NOPE:]
    wuq = jnp.concatenate([uq[:, :, :QK_NOPE], uq_rope, _swap_halves(uq_rope)], axis=-1)
    wuq = wuq.reshape(Q_LORA, H_MLA * QK_PAD).astype(BF16)
    wuk = w_uk[l].reshape(KV_LORA, D_MLA)
    wuv = w_uv[l].reshape(KV_LORA, D_MLA)
    wukv = jnp.concatenate([wuk, wuv], axis=1).astype(BF16)
    gq = jnp.concatenate([g_qn_nope[l], g_qn_rope[l], _swap_halves(g_qn_rope[l])]) * (MLA_SCALE * LOG2E)
    row = lambda a: a.reshape(1, -1).astype(F32)
    consts = [row(g_norm[l]), w_aug, wuq, wukv, row(g_qa[l]), row(jnp.tile(gq, H_MLA)), row(g_kva[l]),
              row(jnp.tile(g_kn_nope[l], H_MLA)),
              row(jnp.concatenate([g_kn_rope[l], _swap_halves(g_kn_rope[l])])),
              row(jnp.tile(g_fq[l], H_FOX) * (FOX_SCALE * LOG2E)), row(jnp.tile(g_fk[l], FOX_KV_HEADS)),
              row(jnp.concatenate([b_f[l], jnp.zeros((LANES - H_FOX,), F32)])),
              _seg_matrix((QK_NOPE, QK_ROPE, QK_ROPE)), _seg_matrix((LANES, LANES)),
              _seg_matrix((QK_ROPE, QK_ROPE))]
    return consts, wuk.T.astype(BF16), wuv.astype(BF16), row(g_kn_nope[l])


def kernel(x_prompt, x_sample, c_prompt, c_sample, cache_mla_ckv, cache_mla_krope, cache_fox_k, cache_fox_v, cache_fox_logf, page_table, g_norm, w_ada, b_ada, w_in, b_f, g_qa, w_uq, g_qn_nope, g_qn_rope, g_kva, g_kn_nope, g_kn_rope, w_uk, w_uv, g_fq, g_fk, w_out):
    b, s, d = x_prompt.shape
    n, t_new, _ = x_sample.shape
    depth, n_pool, page = cache_mla_ckv.shape[:3]
    n_pages = page_table.shape[1]
    past = n_pages * page
    assert t_new == 1 and n % 8 == 0 and s % LANES == 0
    tm = min(256, s)
    tq = min(512, s)
    pages_per_step = min(16, n_pages)
    pad_rows = (-(b + n)) % 8

    tab_p = _rope_tables(jnp.arange(s))
    tab_s = jnp.broadcast_to(_rope_tables(past + jnp.arange(t_new)), (n, 2 * QK_ROPE))

    xp, xs = x_prompt, x_sample
    outs_p, outs_s = [], []
    for l in range(depth):
        consts, wukt, wuv_b, gkn_row = _layer_consts(l, g_norm, w_in, b_f, g_qa, w_uq, g_qn_nope, g_qn_rope,
                                                      g_kva, g_kn_nope, g_kn_rope, w_uk, w_uv, g_fq, g_fk)
        w_out_b = w_out[l].astype(BF16)

        c_all = jnp.concatenate([c_prompt, c_sample, jnp.zeros((pad_rows, d), F32)], axis=0)
        mod = _ada(c_all, w_ada[l], b_ada[l].reshape(1, -1))
        shift_p, scale_p, gate_p = (m.reshape(b, 1, d) for m in jnp.split(mod[:b], 3, axis=-1))
        shift_s, scale_s, gate_s = jnp.split(mod[b:b + n], 3, axis=-1)

        (qcat, ckv, kr, kn, v, krb, fq, fk, fkb, fv, fvb, lf, sz) = _project(
            xp.reshape(b * s, d), shift_p, scale_p, False, tm, consts, tab_p, tab_p)
        lf3 = lf.reshape(b, s, H_FOX)
        f_terms = jnp.stack(_cumsum_lanes(jnp.swapaxes(lf3, 1, 2)), axis=-1)
        qaux, kaux = _fox_bias_lanes(f_terms)
        r3 = lambda a: a.reshape(b, s, a.shape[-1])
        merged = _prompt_attention(r3(qcat), r3(kn), r3(krb), r3(v), r3(fq), qaux, r3(fkb), kaux, r3(fvb),
                                   r3(sz), tq)
        xp = _out_project(xp.reshape(b * s, d), gate_p, merged.reshape(b * s, -1), w_out_b, tq).reshape(b, s, d)
        outs_p.append((ckv.reshape(b, s, KV_LORA), kr.reshape(b, s, QK_ROPE),
                       fk.reshape(b, s, FOX_KV_HEADS, FOX_HD), fv.reshape(b, s, FOX_KV_HEADS, FOX_HD), lf3))

        (qcat_s, ckv_s, kr_s, kn_s, v_s, krb_s, fq_s, fk_s, fkb_s, fv_s, fvb_s, lf_s, sz_s) = _project(
            xs.reshape(n, d), shift_s, scale_s, True, n, consts, tab_s, tab_s)
        qt = jnp.swapaxes(_absorb(qcat_s, gkn_row, wukt), 0, 1)
        qt = jnp.concatenate([qt, jnp.zeros_like(qt)], axis=1)
        q3 = qcat_s.reshape(n, H_MLA, QK_PAD)
        qr = q3[:, :, QK_NOPE:]
        f3 = fq_s.reshape(n, FOX_KV_HEADS, FOX_GROUP, FOX_HD)
        eye = jnp.eye(FOX_KV_HEADS, dtype=BF16)
        fqb = jnp.einsum('ngjd,gk->ngjkd', f3, eye).reshape(n, H_FOX, D_KV_FOX)
        lfn = jnp.broadcast_to(lf_s[:, :, None], (n, H_FOX, LANES))
        fk_c = cache_fox_k.reshape(depth, n_pool, page * FOX_KV_HEADS, FOX_HD)
        fv_c = cache_fox_v.reshape(depth, n_pool, page * FOX_KV_HEADS, FOX_HD)
        stats = _sample_attention(l, page_table, qt, qr, fqb, lfn, wukt, cache_mla_ckv, cache_mla_krope,
                                  fk_c, fv_c, cache_fox_logf, pages_per_step)
        mm, lm, am, mf, lff, af = stats
        hn = lambda a: jnp.swapaxes(a, 0, 1)
        xs = _finish(xs.reshape(n, d), gate_s, qcat_s, kn_s, krb_s, ckv_s, fq_s, fkb_s, fvb_s, sz_s,
                     hn(mm)[:, :, :1], hn(lm)[:, :, :1], hn(am), hn(mf)[:, :, :1], hn(lff)[:, :, :1], hn(af),
                     wuv_b, w_out_b).reshape(n, t_new, d)
        outs_s.append((ckv_s.reshape(n, t_new, KV_LORA), kr_s.reshape(n, t_new, QK_ROPE),
                       fk_s.reshape(n, t_new, FOX_KV_HEADS, FOX_HD), fv_s.reshape(n, t_new, FOX_KV_HEADS, FOX_HD),
                       lf_s.reshape(n, t_new, H_FOX)))

    stack = lambda outs, i: jnp.stack([o[i] for o in outs])
    return (xp, xs, *(stack(outs_p, i) for i in range(5)), *(stack(outs_s, i) for i in range(5)))
```

```python
import functools

import jax
import jax.numpy as jnp
import numpy as np
from jax import lax
from jax.experimental import pallas as pl
from jax.experimental.pallas import tpu as pltpu

F32 = jnp.float32
BF16 = jnp.bfloat16

H_MLA = 8
QK_NOPE = 128
QK_ROPE = 64
QK_HEAD = QK_NOPE + QK_ROPE
QK_PAD = 256
V_DIM = 128
Q_LORA = 512
KV_LORA = 256
D_MLA = H_MLA * V_DIM
H_FOX = 8
FOX_KV_HEADS = 2
FOX_GROUP = H_FOX // FOX_KV_HEADS
FOX_HD = 128
D_FOX = H_FOX * FOX_HD
D_KV_FOX = FOX_KV_HEADS * FOX_HD
ROPE_THETA = 10000.0
EPS = 1e-6
MLA_SCALE = QK_HEAD ** -0.5
FOX_SCALE = FOX_HD ** -0.5
LANES = 128
MXU_DIM = 256
NEG_INF = -1e30
LOG2E = 1.4426950408889634

C_QA = 0
C_KVA = C_QA + Q_LORA
C_KR = C_KVA + KV_LORA
C_ZM = C_KR + 2 * QK_ROPE
C_FQ = C_ZM + D_MLA
C_FK = C_FQ + D_FOX
C_FV = C_FK + D_KV_FOX
C_ZF = C_FV + D_KV_FOX
C_FL = C_ZF + D_FOX
C_END = C_FL + LANES

VMEM_LIMIT = 56 * 1024 * 1024


def _params(*sem):
    return pltpu.CompilerParams(dimension_semantics=sem, vmem_limit_bytes=VMEM_LIMIT)


def _const_spec(shape):
    nd = len(shape)
    return pl.BlockSpec(shape, lambda *_: (0,) * nd, pipeline_mode=pl.Buffered(1))


def _ada_kernel(c_ref, w_ref, b_ref, o_ref):
    c = c_ref[...]
    a = (c / (1.0 + jnp.exp(-c))).astype(BF16)
    o_ref[...] = jnp.dot(a, w_ref[...].astype(BF16), preferred_element_type=F32) + b_ref[...]


def _ada(c_all, w_ada, b_ada):
    rows, d = c_all.shape
    n = w_ada.shape[1]
    tn = 512
    return pl.pallas_call(
        _ada_kernel,
        grid=(n // tn,),
        in_specs=[pl.BlockSpec((rows, d), lambda j: (0, 0)),
                  pl.BlockSpec((d, tn), lambda j: (0, j)),
                  pl.BlockSpec((1, tn), lambda j: (0, j))],
        out_specs=pl.BlockSpec((rows, tn), lambda j: (0, j)),
        out_shape=jax.ShapeDtypeStruct((rows, n), F32),
        compiler_params=_params("arbitrary"),
        name="ada",
    )(c_all, w_ada, b_ada)


def _seg_rms(x, b_ref, gain):
    ms = jnp.dot((x * x).astype(BF16), b_ref[...], preferred_element_type=F32)
    return x * lax.rsqrt(ms + EPS) * gain


def _silu(z):
    return z / (1.0 + jnp.exp(-z))


def _proj_kernel(x_ref, shift_ref, scale_ref, gnorm_ref, win_ref, wuq_ref, wukv_ref,
                 gqa_ref, gq_ref, gkva_ref, gkn_ref, gkr_ref, gfq_ref, gfk_ref, bf_ref,
                 bq_ref, b128_ref, bkr_ref, tabq_ref, tabk_ref,
                 qcat_o, ckv_o, kr_o, kn_o, v_o, krb_o, fq_o, fk_o, fkb_o, fv_o, fvb_o, lf_o, sz_o):
    x = x_ref[...]
    ms = jnp.mean(x * x, axis=-1, keepdims=True)
    xn = x * lax.rsqrt(ms + EPS) * gnorm_ref[...]
    h = xn * (1.0 + scale_ref[...]) + shift_ref[...]
    hb = h.astype(BF16)

    def proj(lo, hi):
        return jnp.dot(hb, win_ref[:, lo:hi], preferred_element_type=F32)

    lane = lax.broadcasted_iota(jnp.int32, (x.shape[0], LANES), 1)

    qa = proj(C_QA, C_KVA)
    qn = qa * lax.rsqrt(jnp.mean(qa * qa, axis=-1, keepdims=True) + EPS) * gqa_ref[...]
    q_raw = jnp.dot(qn.astype(BF16), wuq_ref[...], preferred_element_type=F32)
    tabq = tabq_ref[...]
    for hd in range(H_MLA):
        lo = hd * QK_PAD
        y = _seg_rms(q_raw[:, lo:lo + QK_PAD], bq_ref, gq_ref[:, lo:lo + QK_PAD])
        t = y[:, LANES:] * tabq
        rot = t + pltpu.roll(t, QK_ROPE, axis=1)
        qcat_o[:, lo:lo + LANES] = y[:, :LANES].astype(BF16)
        qcat_o[:, lo + LANES:lo + QK_PAD] = jnp.where(lane < QK_ROPE, rot, 0.0).astype(BF16)

    kva = proj(C_KVA, C_KR)
    ckv = kva * lax.rsqrt(jnp.mean(kva * kva, axis=-1, keepdims=True) + EPS) * gkva_ref[...]
    ckv_o[...] = ckv
    kv_raw = jnp.dot(ckv.astype(BF16), wukv_ref[...], preferred_element_type=F32)
    for c in range(D_MLA // MXU_DIM):
        lo = c * MXU_DIM
        kn_o[:, lo:lo + MXU_DIM] = _seg_rms(kv_raw[:, lo:lo + MXU_DIM], b128_ref,
                                            gkn_ref[:, lo:lo + MXU_DIM]).astype(BF16)
    v_o[...] = kv_raw[:, D_MLA:].astype(BF16)

    kr = _seg_rms(proj(C_KR, C_ZM), bkr_ref, gkr_ref[...]) * tabk_ref[...]
    kr = kr + pltpu.roll(kr, QK_ROPE, axis=1)
    kr_o[...] = kr[:, :QK_ROPE]
    krb_o[...] = jnp.where(lane < QK_ROPE, kr, 0.0).astype(BF16)

    fq_raw = proj(C_FQ, C_FK)
    for c in range(D_FOX // MXU_DIM):
        lo = c * MXU_DIM
        fq_o[:, lo:lo + MXU_DIM] = _seg_rms(fq_raw[:, lo:lo + MXU_DIM], b128_ref,
                                            gfq_ref[:, lo:lo + MXU_DIM]).astype(BF16)
    fk = _seg_rms(proj(C_FK, C_FV), b128_ref, gfk_ref[...])
    fk_o[...] = fk
    fkb_o[...] = fk.astype(BF16)
    fv = proj(C_FV, C_ZF)
    fv_o[...] = fv
    fvb_o[...] = fv.astype(BF16)
    fl = proj(C_FL, C_END) + bf_ref[...]
    lf = jnp.minimum(fl, 0.0) - jnp.log(1.0 + jnp.exp(-jnp.abs(fl)))
    lf_o[...] = lf[:, :H_FOX]

    sz_o[:, :D_MLA] = _silu(proj(C_ZM, C_FQ)).astype(BF16)
    sz_o[:, D_MLA:] = _silu(proj(C_ZF, C_FL)).astype(BF16)


def _project(x2d, shift, scale, per_row_mod, tm, consts, tabq, tabk):
    rows, d = x2d.shape
    nt = rows // tm
    if per_row_mod:
        mod_spec = pl.BlockSpec((tm, d), lambda r: (r, 0))
    else:
        tiles_per_b = nt // shift.shape[0]
        mod_spec = pl.BlockSpec((None, 1, d), lambda r: (r // tiles_per_b, 0, 0))
    tab_tiles = tabq.shape[0] // tm
    row = lambda w: pl.BlockSpec((tm, w), lambda r: (r, 0))
    out_widths = [(H_MLA * QK_PAD, BF16), (KV_LORA, F32), (QK_ROPE, F32), (D_MLA, BF16), (D_MLA, BF16),
                  (LANES, BF16), (D_FOX, BF16), (D_KV_FOX, F32), (D_KV_FOX, BF16), (D_KV_FOX, F32),
                  (D_KV_FOX, BF16), (H_FOX, F32), (D_MLA + D_FOX, BF16)]
    in_specs = ([row(d), mod_spec, mod_spec] + [_const_spec(c.shape) for c in consts]
                + [pl.BlockSpec((tm, QK_PAD - LANES), lambda r: (r % tab_tiles, 0)),
                   pl.BlockSpec((tm, LANES), lambda r: (r % tab_tiles, 0))])
    return pl.pallas_call(
        _proj_kernel,
        grid=(nt,),
        in_specs=in_specs,
        out_specs=[row(w) for w, _ in out_widths],
        out_shape=[jax.ShapeDtypeStruct((rows, w), dt) for w, dt in out_widths],
        compiler_params=_params("arbitrary"),
        name="proj",
    )(x2d, shift, scale, *consts, tabq, tabk)


def _split3(x):
    hi = x.astype(BF16)
    r1 = x - hi.astype(F32)
    mid = r1.astype(BF16)
    lo = (r1 - mid.astype(F32)).astype(BF16)
    return hi, mid, lo


def _dot3(x, m):
    hi, mid, lo = _split3(x)
    d = lambda a: jnp.dot(a, m, preferred_element_type=F32)
    return d(hi) + (d(mid) + d(lo))


def _cumsum_kernel(x_ref, tri_ref, hi_ref, mid_ref, lo_ref):
    nblk = x_ref.shape[1] // LANES
    carry = jnp.zeros((x_ref.shape[0], 1), F32)
    for i in range(nblk):
        sl = slice(i * LANES, (i + 1) * LANES)
        blk = _dot3(x_ref[:, sl], tri_ref[...]) + carry
        carry = blk[:, LANES - 1:LANES]
        hi_ref[:, sl], mid_ref[:, sl], lo_ref[:, sl] = _split3(blk * LOG2E)


def _cumsum_lanes(x_t):
    b, hh, s = x_t.shape
    tri = jnp.asarray(np.triu(np.ones((LANES, LANES), np.float32)), BF16)
    spec = pl.BlockSpec((None, hh, s), lambda i: (i, 0, 0))
    return pl.pallas_call(
        _cumsum_kernel,
        grid=(b,),
        in_specs=[spec, _const_spec((LANES, LANES))],
        out_specs=[spec] * 3,
        out_shape=[jax.ShapeDtypeStruct((b, hh, s), BF16)] * 3,
        compiler_params=_params("arbitrary"),
        name="cumsum",
    )(x_t, tri)


def _fox_bias_lanes(f_terms):
    b, hh, s, nt = f_terms.shape
    w = hh * nt
    f = jnp.transpose(f_terms, (0, 2, 1, 3)).reshape(b, s, w)
    zeros = lambda k: jnp.zeros((b, s, k), BF16)
    qaux = jnp.concatenate([f, zeros(LANES - w)], axis=-1)
    kaux = jnp.concatenate([jnp.ones((b, s, w), BF16), -f, zeros(LANES - 2 * w)], axis=-1)
    own = np.repeat(np.eye(hh, dtype=np.float32), nt, axis=1)
    pad = np.zeros((hh, LANES - 2 * w), np.float32)
    qmask = jnp.asarray(np.concatenate([own, np.zeros_like(own), pad], axis=1), BF16)
    qones = jnp.asarray(np.concatenate([np.zeros_like(own), own, pad], axis=1), BF16)
    return qaux, kaux, qmask, qones


def _nt_dot(a, b):
    return lax.dot_general(a, b, (((1,), (1,)), ((), ())), preferred_element_type=F32)


def _p_attn_kernel(qcat_ref, kn_ref, krb_ref, v_ref, fq_ref, qaux_ref, fkb_ref, kaux_ref, fvb_ref, sz_ref,
                   qmask_ref, qones_ref, o_ref, m_m, acc_m, m_f, acc_f):
    qi = pl.program_id(1)
    ki = pl.program_id(2)
    tq = qcat_ref.shape[0]
    tk = kn_ref.shape[0]
    acc_w = V_DIM + LANES

    @pl.when(ki == 0)
    def _():
        m_m[...] = jnp.full_like(m_m, NEG_INF)
        m_f[...] = jnp.full_like(m_f, NEG_INF)
        acc_m[...] = jnp.zeros_like(acc_m)
        acc_f[...] = jnp.zeros_like(acc_f)

    def update(s, hd, m_ref, acc_ref, vv, ones):
        m_prev = m_ref[hd]
        m_new = jnp.maximum(m_prev, jnp.max(s, axis=-1, keepdims=True))
        alpha = jnp.exp2(m_prev - m_new)
        p = jnp.exp2(s - jnp.concatenate([m_new] * (tk // LANES), axis=1)).astype(BF16)
        pv = jnp.dot(p, jnp.concatenate([vv, ones], axis=1), preferred_element_type=F32)
        sl = slice(hd * acc_w, (hd + 1) * acc_w)
        acc_ref[:, sl] = jnp.concatenate([alpha] * (acc_w // LANES), axis=1) * acc_ref[:, sl] + pv
        m_ref[hd] = m_new

    def step(masked):
        if masked:
            keep = (lax.broadcasted_iota(jnp.int32, (tq, tk), 1)
                    <= lax.broadcasted_iota(jnp.int32, (tq, tk), 0))
        ones = jnp.ones((tk, LANES), BF16)
        krb = krb_ref[...]
        for hd in range(H_MLA):
            kcat = jnp.concatenate([kn_ref[:, hd * QK_NOPE:(hd + 1) * QK_NOPE], krb], axis=1)
            s = _nt_dot(qcat_ref[:, hd * QK_PAD:(hd + 1) * QK_PAD], kcat)
            if masked:
                s = jnp.where(keep, s, NEG_INF)
            update(s, hd, m_m, acc_m, v_ref[:, hd * V_DIM:(hd + 1) * V_DIM], ones)
        qaux = qaux_ref[...]
        kaux = kaux_ref[...]
        for hd in range(H_FOX):
            g = hd // FOX_GROUP
            hs = slice(hd * FOX_HD, (hd + 1) * FOX_HD)
            gs = slice(g * FOX_HD, (g + 1) * FOX_HD)
            qa = qaux * qmask_ref[hd:hd + 1, :] + qones_ref[hd:hd + 1, :]
            s = _nt_dot(jnp.concatenate([fq_ref[:, hs], qa], axis=1),
                        jnp.concatenate([fkb_ref[:, gs], kaux], axis=1))
            if masked:
                s = jnp.where(keep, s, NEG_INF)
            update(s, hd, m_f, acc_f, fvb_ref[:, gs], ones)

    @pl.when(ki < qi)
    def _():
        step(False)

    @pl.when(ki == qi)
    def _():
        step(True)
        for acc_ref, base in ((acc_m, 0), (acc_f, D_MLA)):
            for hd in range(H_MLA):
                lo = hd * acc_w
                so = slice(base + hd * V_DIM, base + (hd + 1) * V_DIM)
                o = acc_ref[:, lo:lo + V_DIM] / acc_ref[:, lo + V_DIM:lo + acc_w]
                o_ref[:, so] = (o * sz_ref[:, so].astype(F32)).astype(BF16)


def _prompt_attention(qcat, kn, krb, v, fq, qaux, fkb, kaux, fvb, sz, qmask, qones, tq):
    b, s, _ = qcat.shape
    nq = s // tq
    qmap = lambda bi, qi, ki: (bi, qi, 0)
    kmap = lambda bi, qi, ki: (bi, jnp.minimum(ki, qi), 0)
    qs = lambda w: pl.BlockSpec((None, tq, w), qmap)
    ks = lambda w: pl.BlockSpec((None, tq, w), kmap)
    acc_w = V_DIM + LANES
    return pl.pallas_call(
        _p_attn_kernel,
        grid=(b, nq, nq),
        in_specs=[qs(H_MLA * QK_PAD), ks(D_MLA), ks(LANES), ks(D_MLA), qs(D_FOX), qs(LANES),
                  ks(D_KV_FOX), ks(LANES), ks(D_KV_FOX), qs(D_MLA + D_FOX),
                  pl.BlockSpec((H_FOX, LANES), lambda bi, qi, ki: (0, 0)),
                  pl.BlockSpec((H_FOX, LANES), lambda bi, qi, ki: (0, 0))],
        out_specs=qs(D_MLA + D_FOX),
        out_shape=jax.ShapeDtypeStruct((b, s, D_MLA + D_FOX), BF16),
        scratch_shapes=[pltpu.VMEM((H_MLA, tq, LANES), F32), pltpu.VMEM((tq, H_MLA * acc_w), F32),
                        pltpu.VMEM((H_FOX, tq, LANES), F32), pltpu.VMEM((tq, H_FOX * acc_w), F32)],
        compiler_params=_params("arbitrary", "arbitrary", "arbitrary"),
        name="p_attn",
    )(qcat, kn, krb, v, fq, qaux, fkb, kaux, fvb, sz, qmask, qones)


def _out_kernel(x_ref, gate_ref, merged_ref, w_ref, o_ref):
    y = jnp.dot(merged_ref[...], w_ref[...], preferred_element_type=F32)
    o_ref[...] = x_ref[...] + gate_ref[...] * y


def _out_project(x2d, gate, merged, w_out_b, tm):
    rows, d = x2d.shape
    nt = rows // tm
    tiles_per_b = nt // gate.shape[0]
    return pl.pallas_call(
        _out_kernel,
        grid=(nt,),
        in_specs=[pl.BlockSpec((tm, d), lambda r: (r, 0)),
                  pl.BlockSpec((None, 1, d), lambda r: (r // tiles_per_b, 0, 0)),
                  pl.BlockSpec((tm, merged.shape[1]), lambda r: (r, 0)),
                  _const_spec(w_out_b.shape)],
        out_specs=pl.BlockSpec((tm, d), lambda r: (r, 0)),
        out_shape=jax.ShapeDtypeStruct((rows, d), F32),
        compiler_params=_params("arbitrary"),
        name="out_proj",
    )(x2d, gate, merged, w_out_b)


def _absorb_kernel(qcat_ref, gkn_ref, wukt_ref, o_ref):
    for hd in range(H_MLA):
        qn = qcat_ref[:, hd * QK_PAD:hd * QK_PAD + QK_NOPE].astype(F32) * gkn_ref[...]
        o_ref[hd] = jnp.dot(qn.astype(BF16), wukt_ref[hd * QK_NOPE:(hd + 1) * QK_NOPE, :],
                            preferred_element_type=F32).astype(BF16)


def _absorb(qcat_s, gkn, wukt):
    n = qcat_s.shape[0]
    return pl.pallas_call(
        _absorb_kernel,
        grid=(1,),
        in_specs=[_const_spec(qcat_s.shape), _const_spec(gkn.shape), _const_spec(wukt.shape)],
        out_specs=_const_spec((H_MLA, n, KV_LORA)),
        out_shape=jax.ShapeDtypeStruct((H_MLA, n, KV_LORA), BF16),
        compiler_params=_params("arbitrary"),
        name="absorb",
    )(qcat_s, gkn, wukt)


def _s_attn_kernel(pt_ref, qt_ref, qr_ref, fq_ref, lfn_ref, wukt_ref, sfx_ref,
                   ckv_hbm, kr_hbm, fk_hbm, fv_hbm, lf_hbm,
                   mm_o, lm_o, am_o, mf_o, lf_o, af_o,
                   cbuf, kbuf, fkbuf, fvbuf, lbuf, sems, cb, krb, fkb, fvb, carry,
                   *, layer, pages, page, nc, n):
    i = pl.program_id(0)
    c = pl.program_id(1)
    g = i * nc + c
    slot = lax.rem(g, 2)
    t = pages * page
    rows_kv = page * FOX_KV_HEADS

    def page_copies(seq, chunk, slot_):
        cps = []
        for j in range(pages):
            pid = pt_ref[seq, (nc - 1 - chunk) * pages + j]
            rows = pl.ds(j * page, page)
            rows2 = pl.ds(j * rows_kv, rows_kv)
            for k, (src, dst) in enumerate(((ckv_hbm, cbuf.at[slot_, rows]), (kr_hbm, kbuf.at[slot_, :, rows]),
                                            (fk_hbm, fkbuf.at[slot_, rows2]), (fv_hbm, fvbuf.at[slot_, rows2]),
                                            (lf_hbm, lbuf.at[slot_, pl.ds(j * H_FOX, H_FOX)]))):
                cps.append(pltpu.make_async_copy(src.at[layer, pid], dst, sems.at[k, slot_]))
        return cps

    @pl.when(g == 0)
    def _():
        krb[...] = jnp.zeros_like(krb)
        for cp in page_copies(0, 0, 0):
            cp.start()

    @pl.when(c == 0)
    def _():
        mm_o[...] = jnp.full_like(mm_o, NEG_INF)
        mf_o[...] = jnp.full_like(mf_o, NEG_INF)
        lm_o[...] = jnp.zeros_like(lm_o)
        lf_o[...] = jnp.zeros_like(lf_o)
        am_o[...] = jnp.zeros_like(am_o)
        af_o[...] = jnp.zeros_like(af_o)
        carry[...] = lfn_ref[...] * LOG2E

    for cp in page_copies(i, c, slot):
        cp.wait()
    last_c = c == nc - 1
    c_next = jnp.where(last_c, 0, c + 1)
    i_next = jnp.where(last_c, jnp.where(i == n - 1, 0, i + 1), i)
    nxt = page_copies(i_next, c_next, 1 - slot)
    per_page = len(nxt) // pages
    for j in range(pages):
        for cp in nxt[j * per_page:(j + 1) * per_page]:
            cp.start()
        rows = slice(j * page, (j + 1) * page)
        cb[rows, :] = cbuf[slot, rows, :].astype(BF16)
        krb[0:QK_ROPE, rows] = kbuf[slot, :, rows].astype(BF16)
        for gg in range(FOX_KV_HEADS):
            src_rows = pl.ds(j * rows_kv + gg, page, stride=FOX_KV_HEADS)
            fkb[gg, rows, :] = fkbuf[slot, src_rows, :].astype(BF16)
            fvb[gg, rows, :] = fvbuf[slot, src_rows, :].astype(BF16)

    def update(s, m_o, l_o):
        m_prev = m_o[...]
        m_new = jnp.maximum(m_prev, jnp.max(s, axis=-1, keepdims=True))
        alpha = jnp.exp2(m_prev - m_new)
        p = jnp.exp2(s - jnp.concatenate([m_new] * (t // LANES), axis=1))
        l_o[...] = alpha * l_o[...] + jnp.sum(p, axis=-1, keepdims=True)
        m_o[...] = m_new
        return jnp.concatenate([alpha, alpha], axis=1), p.astype(BF16)

    lhs = jnp.concatenate([wukt_ref[...], qt_ref[...]], axis=0)
    sub = min(t, 512)
    n_sub = t // sub

    def mla_logits(u):
        kt = _nt_dot(lhs, cb[u * sub:(u + 1) * sub, :])
        ssq = jnp.concatenate(
            [jnp.sum(kt[hd * QK_NOPE:(hd + 1) * QK_NOPE, :] ** 2, axis=0, keepdims=True) for hd in range(H_MLA)],
            axis=0)
        return kt[H_MLA * QK_NOPE:H_MLA * QK_NOPE + H_MLA, :] * lax.rsqrt(ssq * (1.0 / QK_NOPE) + EPS)

    def fox_logits():
        sfx = _dot3(lbuf[slot] * LOG2E, sfx_ref[...])
        run = carry[...]
        bias = [None] * pages
        for j in reversed(range(pages)):
            blk = sfx[j * H_FOX:(j + 1) * H_FOX, :]
            bias[j] = blk[:, :LANES] + run
            run = run + blk[:, LANES:]
        carry[...] = run
        fqv = fq_ref[...]
        s_f = jnp.concatenate(bias, axis=1)
        for gg in range(FOX_KV_HEADS):
            s_f = s_f + _nt_dot(fqv[:, gg * FOX_HD:(gg + 1) * FOX_HD], fkb[gg])
        return s_f

    def fox_values(alpha, p):
        pv = jnp.concatenate([jnp.dot(p, fvb[gg], preferred_element_type=F32) for gg in range(FOX_KV_HEADS)],
                             axis=1)
        af_o[...] = alpha * af_o[...] + pv

    s_m = []
    fox_p = None
    logits_at, values_at = min(1, n_sub - 1), min(2, n_sub - 1)
    for u in range(n_sub):
        s_m.append(mla_logits(u))
        if u == logits_at:
            fox_p = update(fox_logits(), mf_o, lf_o)
        if u == values_at:
            fox_values(*fox_p)
    s_m = jnp.concatenate(s_m, axis=1) + jnp.dot(qr_ref[...], krb[...], preferred_element_type=F32)
    alpha, p = update(s_m, mm_o, lm_o)
    am_o[...] = alpha * am_o[...] + jnp.dot(p, cb[...], preferred_element_type=F32)

    @pl.when(g == n * nc - 1)
    def _():
        for cp in page_copies(0, 0, 1 - slot):
            cp.wait()


def _sample_attention(layer, page_table, qt, qr, fqb, lfn, wukt, ckv_c, kr_c, fk_c, fv_c, lf_c, pages):
    n, n_pages = page_table.shape
    page = ckv_c.shape[2]
    assert page == LANES
    nc = n_pages // pages
    t = pages * page
    idx = np.arange(page)
    sfx = np.concatenate([(idx[:, None] > idx[None, :]).astype(np.float32),
                          np.ones((page, LANES), np.float32)], axis=1)
    sfx = jnp.asarray(sfx, BF16)

    seq = lambda shape: pl.BlockSpec((None,) + shape, lambda i, c, pt: (i,) + (0,) * len(shape))
    cst = lambda shape: pl.BlockSpec(shape, lambda i, c, pt: (0,) * len(shape))
    hbm = pl.BlockSpec(memory_space=pl.ANY)
    in_specs = [seq((2 * H_MLA, KV_LORA)), seq((H_MLA, LANES)), seq((H_FOX, D_KV_FOX)), seq((H_FOX, LANES)),
                cst(wukt.shape), cst(sfx.shape), hbm, hbm, hbm, hbm, hbm]
    stat = jax.ShapeDtypeStruct((n, H_MLA, LANES), F32)
    accs = jax.ShapeDtypeStruct((n, H_MLA, KV_LORA), F32)
    n_arrays = 5
    grid_spec = pltpu.PrefetchScalarGridSpec(
        num_scalar_prefetch=1,
        grid=(n, nc),
        in_specs=in_specs,
        out_specs=[seq((H_MLA, LANES)), seq((H_MLA, LANES)), seq((H_MLA, KV_LORA)),
                   seq((H_FOX, LANES)), seq((H_FOX, LANES)), seq((H_FOX, D_KV_FOX))],
        scratch_shapes=[pltpu.VMEM((2, t, KV_LORA), F32), pltpu.VMEM((2, QK_ROPE, t), F32),
                        pltpu.VMEM((2, t * FOX_KV_HEADS, FOX_HD), F32),
                        pltpu.VMEM((2, t * FOX_KV_HEADS, FOX_HD), F32),
                        pltpu.VMEM((2, pages * H_FOX, page), F32), pltpu.SemaphoreType.DMA((n_arrays, 2)),
                        pltpu.VMEM((t, KV_LORA), BF16), pltpu.VMEM((LANES, t), BF16),
                        pltpu.VMEM((FOX_KV_HEADS, t, FOX_HD), BF16), pltpu.VMEM((FOX_KV_HEADS, t, FOX_HD), BF16),
                        pltpu.VMEM((H_FOX, LANES), F32)],
    )
    return pl.pallas_call(
        functools.partial(_s_attn_kernel, layer=layer, pages=pages, page=page, nc=nc, n=n),
        grid_spec=grid_spec,
        out_shape=[stat, stat, accs, stat, stat, accs],
        compiler_params=_params("arbitrary", "arbitrary"),
        name="s_attn",
    )(page_table, qt, qr, fqb, lfn, wukt, sfx, ckv_c, kr_c, fk_c, fv_c, lf_c)


def _finish_kernel(x_ref, gate_ref, qcat_ref, kn_ref, krb_ref, ckv_ref, fq_ref, fkb_ref, fvb_ref, sz_ref,
                   mm_ref, lm_ref, am_ref, mf_ref, lf_ref, af_ref, wuv_ref, wout_ref, o_ref, merged):
    krb = krb_ref[...].astype(F32)
    cself = ckv_ref[...].astype(BF16).astype(F32)

    def merge(s_self, m_past, l_past):
        m_tot = jnp.maximum(m_past, s_self)
        a = jnp.exp2(m_past - m_tot)
        b = jnp.exp2(s_self - m_tot)
        inv = 1.0 / (a * l_past + b)
        return a * inv, b * inv

    for hd in range(H_MLA):
        q = qcat_ref[:, hd * QK_PAD:(hd + 1) * QK_PAD].astype(F32)
        s_self = (jnp.sum(q[:, :QK_NOPE] * kn_ref[:, hd * QK_NOPE:(hd + 1) * QK_NOPE].astype(F32),
                          axis=-1, keepdims=True)
                  + jnp.sum(q[:, QK_NOPE:] * krb, axis=-1, keepdims=True))
        wa, wb = merge(s_self, mm_ref[hd], lm_ref[hd])
        lat = wa * am_ref[hd] + wb * cself
        o = jnp.dot(lat.astype(BF16), wuv_ref[:, hd * V_DIM:(hd + 1) * V_DIM], preferred_element_type=F32)
        sl = slice(hd * V_DIM, (hd + 1) * V_DIM)
        merged[:, sl] = (o * sz_ref[:, sl].astype(F32)).astype(BF16)
    for hd in range(H_FOX):
        g = hd // FOX_GROUP
        gs = slice(g * FOX_HD, (g + 1) * FOX_HD)
        s_self = jnp.sum(fq_ref[:, hd * FOX_HD:(hd + 1) * FOX_HD].astype(F32) * fkb_ref[:, gs].astype(F32),
                         axis=-1, keepdims=True)
        wa, wb = merge(s_self, mf_ref[hd], lf_ref[hd])
        o = wa * af_ref[hd][:, gs] + wb * fvb_ref[:, gs].astype(F32)
        so = slice(D_MLA + hd * FOX_HD, D_MLA + (hd + 1) * FOX_HD)
        merged[:, so] = (o * sz_ref[:, so].astype(F32)).astype(BF16)
    y = jnp.dot(merged[...], wout_ref[...], preferred_element_type=F32)
    o_ref[...] = x_ref[...] + gate_ref[...] * y


def _finish(*args):
    x = args[0]
    return pl.pallas_call(
        _finish_kernel,
        grid=(1,),
        in_specs=[_const_spec(a.shape) for a in args],
        out_specs=_const_spec(x.shape),
        out_shape=jax.ShapeDtypeStruct(x.shape, F32),
        scratch_shapes=[pltpu.VMEM((x.shape[0], D_MLA + D_FOX), BF16)],
        compiler_params=_params("arbitrary"),
        name="finish",
    )(*args)


def _swap_halves(a, axis=-1):
    lo, hi = jnp.split(a, 2, axis=axis)
    return jnp.concatenate([hi, lo], axis=axis)


def _seg_matrix(widths):
    n = sum(widths)
    m = np.zeros((n, n), np.float32)
    start = 0
    for w in widths:
        m[start:start + w, start:start + w] = 1.0 / w
        start += w
    return jnp.asarray(m, BF16)


def _rope_tables(pos):
    inv_freq = ROPE_THETA ** (-jnp.arange(0, QK_ROPE, 2, dtype=F32) / QK_ROPE)
    ang = pos.astype(F32)[:, None] * inv_freq[None, :]
    cos, sin = jnp.cos(ang), jnp.sin(ang)
    return jnp.concatenate([cos, cos, -sin, sin], axis=-1)


def _layer_consts(l, g_norm, w_in, b_f, g_qa, w_uq, g_qn_nope, g_qn_rope, g_kva, g_kn_nope, g_kn_rope,
                  w_uk, w_uv, g_fq, g_fk):
    w = w_in[l]
    d = w.shape[0]
    widths = (Q_LORA, KV_LORA, QK_ROPE, D_MLA, D_FOX, D_KV_FOX, D_KV_FOX, H_FOX, D_FOX)
    offs = np.concatenate([[0], np.cumsum(widths)])
    part = lambda i: w[:, offs[i]:offs[i + 1]]
    w_aug = jnp.concatenate([part(0), part(1), part(2), _swap_halves(part(2)), part(3), part(4), part(5),
                             part(6), part(8), part(7), jnp.zeros((d, LANES - H_FOX), F32)], axis=1).astype(BF16)
    uq = w_uq[l].reshape(Q_LORA, H_MLA, QK_HEAD)
    uq_rope = uq[:, :, QK_NOPE:]
    wuq = jnp.concatenate([uq[:, :, :QK_NOPE], uq_rope, _swap_halves(uq_rope)], axis=-1)
    wuq = wuq.reshape(Q_LORA, H_MLA * QK_PAD).astype(BF16)
    wuk = w_uk[l].reshape(KV_LORA, D_MLA)
    wuv = w_uv[l].reshape(KV_LORA, D_MLA)
    wukv = jnp.concatenate([wuk, wuv], axis=1).astype(BF16)
    gq = jnp.concatenate([g_qn_nope[l], g_qn_rope[l], _swap_halves(g_qn_rope[l])]) * (MLA_SCALE * LOG2E)
    row = lambda a: a.reshape(1, -1).astype(F32)
    consts = [row(g_norm[l]), w_aug, wuq, wukv, row(g_qa[l]), row(jnp.tile(gq, H_MLA)), row(g_kva[l]),
              row(jnp.tile(g_kn_nope[l], H_MLA)),
              row(jnp.concatenate([g_kn_rope[l], _swap_halves(g_kn_rope[l])])),
              row(jnp.tile(g_fq[l], H_FOX) * (FOX_SCALE * LOG2E)), row(jnp.tile(g_fk[l], FOX_KV_HEADS)),
              row(jnp.concatenate([b_f[l], jnp.zeros((LANES - H_FOX,), F32)])),
              _seg_matrix((QK_NOPE, QK_ROPE, QK_ROPE)), _seg_matrix((LANES, LANES)),
              _seg_matrix((QK_ROPE, QK_ROPE))]
    return consts, wuk.T.astype(BF16), wuv.astype(BF16), row(g_kn_nope[l])


def kernel(x_prompt, x_sample, c_prompt, c_sample, cache_mla_ckv, cache_mla_krope, cache_fox_k, cache_fox_v, cache_fox_logf, page_table, g_norm, w_ada, b_ada, w_in, b_f, g_qa, w_uq, g_qn_nope, g_qn_rope, g_kva, g_kn_nope, g_kn_rope, w_uk, w_uv, g_fq, g_fk, w_out):
    b, s, d = x_prompt.shape
    n, t_new, _ = x_sample.shape
    depth, n_pool, page = cache_mla_ckv.shape[:3]
    n_pages = page_table.shape[1]
    past = n_pages * page
    assert t_new == 1 and n % 8 == 0 and s % LANES == 0
    tm = min(256, s)
    tq = min(512, s)
    pages_per_step = min(16, n_pages)
    pad_rows = (-(b + n)) % 8

    tab_p = _rope_tables(jnp.arange(s))
    tab_s = jnp.broadcast_to(_rope_tables(past + jnp.arange(t_new)), (n, 2 * QK_ROPE))

    xp, xs = x_prompt, x_sample
    outs_p, outs_s = [], []
    for l in range(depth):
        consts, wukt, wuv_b, gkn_row = _layer_consts(l, g_norm, w_in, b_f, g_qa, w_uq, g_qn_nope, g_qn_rope,
                                                      g_kva, g_kn_nope, g_kn_rope, w_uk, w_uv, g_fq, g_fk)
        w_out_b = w_out[l].astype(BF16)

        c_all = jnp.concatenate([c_prompt, c_sample, jnp.zeros((pad_rows, d), F32)], axis=0)
        mod = _ada(c_all, w_ada[l], b_ada[l].reshape(1, -1))
        shift_p, scale_p, gate_p = (m.reshape(b, 1, d) for m in jnp.split(mod[:b], 3, axis=-1))
        shift_s, scale_s, gate_s = jnp.split(mod[b:b + n], 3, axis=-1)

        (qcat, ckv, kr, kn, v, krb, fq, fk, fkb, fv, fvb, lf, sz) = _project(
            xp.reshape(b * s, d), shift_p, scale_p, False, tm, consts, tab_p, tab_p)
        lf3 = lf.reshape(b, s, H_FOX)
        f_terms = jnp.stack(_cumsum_lanes(jnp.swapaxes(lf3, 1, 2)), axis=-1)
        qaux, kaux, qmask, qones = _fox_bias_lanes(f_terms)
        r3 = lambda a: a.reshape(b, s, a.shape[-1])
        merged = _prompt_attention(r3(qcat), r3(kn), r3(krb), r3(v), r3(fq), qaux, r3(fkb), kaux, r3(fvb),
                                   r3(sz), qmask, qones, tq)
        xp = _out_project(xp.reshape(b * s, d), gate_p, merged.reshape(b * s, -1), w_out_b, tq).reshape(b, s, d)
        outs_p.append((ckv.reshape(b, s, KV_LORA), kr.reshape(b, s, QK_ROPE),
                       fk.reshape(b, s, FOX_KV_HEADS, FOX_HD), fv.reshape(b, s, FOX_KV_HEADS, FOX_HD), lf3))

        (qcat_s, ckv_s, kr_s, kn_s, v_s, krb_s, fq_s, fk_s, fkb_s, fv_s, fvb_s, lf_s, sz_s) = _project(
            xs.reshape(n, d), shift_s, scale_s, True, n, consts, tab_s, tab_s)
        qt = jnp.swapaxes(_absorb(qcat_s, gkn_row, wukt), 0, 1)
        qt = jnp.concatenate([qt, jnp.zeros_like(qt)], axis=1)
        q3 = qcat_s.reshape(n, H_MLA, QK_PAD)
        qr = q3[:, :, QK_NOPE:]
        f3 = fq_s.reshape(n, FOX_KV_HEADS, FOX_GROUP, FOX_HD)
        eye = jnp.eye(FOX_KV_HEADS, dtype=BF16)
        fqb = jnp.einsum('ngjd,gk->ngjkd', f3, eye).reshape(n, H_FOX, D_KV_FOX)
        lfn = jnp.broadcast_to(lf_s[:, :, None], (n, H_FOX, LANES))
        fk_c = cache_fox_k.reshape(depth, n_pool, page * FOX_KV_HEADS, FOX_HD)
        fv_c = cache_fox_v.reshape(depth, n_pool, page * FOX_KV_HEADS, FOX_HD)
        kr_c = jnp.swapaxes(cache_mla_krope, 2, 3)
        lf_c = jnp.swapaxes(cache_fox_logf, 2, 3)
        stats = _sample_attention(l, page_table, qt, qr, fqb, lfn, wukt, cache_mla_ckv, kr_c,
                                  fk_c, fv_c, lf_c, pages_per_step)
        mm, lm, am, mf, lff, af = stats
        hn = lambda a: jnp.swapaxes(a, 0, 1)
        xs = _finish(xs.reshape(n, d), gate_s, qcat_s, kn_s, krb_s, ckv_s, fq_s, fkb_s, fvb_s, sz_s,
                     hn(mm)[:, :, :1], hn(lm)[:, :, :1], hn(am), hn(mf)[:, :, :1], hn(lff)[:, :, :1], hn(af),
                     wuv_b, w_out_b).reshape(n, t_new, d)
        outs_s.append((ckv_s.reshape(n, t_new, KV_LORA), kr_s.reshape(n, t_new, QK_ROPE),
                       fk_s.reshape(n, t_new, FOX_KV_HEADS, FOX_HD), fv_s.reshape(n, t_new, FOX_KV_HEADS, FOX_HD),
                       lf_s.reshape(n, t_new, H_FOX)))

    stack = lambda outs, i: jnp.stack([o[i] for o in outs])
    return (xp, xs, *(stack(outs_p, i) for i in range(5)), *(stack(outs_s, i) for i in range(5)))
```

```python
import functools

import jax
import jax.numpy as jnp
import numpy as np
from jax import lax
from jax.experimental import pallas as pl
from jax.experimental.pallas import tpu as pltpu

F32 = jnp.float32
BF16 = jnp.bfloat16

H_MLA = 8
QK_NOPE = 128
QK_ROPE = 64
QK_HEAD = QK_NOPE + QK_ROPE
QK_PAD = 256
V_DIM = 128
Q_LORA = 512
KV_LORA = 256
D_MLA = H_MLA * V_DIM
H_FOX = 8
FOX_KV_HEADS = 2
FOX_GROUP = H_FOX // FOX_KV_HEADS
FOX_HD = 128
D_FOX = H_FOX * FOX_HD
D_KV_FOX = FOX_KV_HEADS * FOX_HD
ROPE_THETA = 10000.0
EPS = 1e-6
MLA_SCALE = QK_HEAD ** -0.5
FOX_SCALE = FOX_HD ** -0.5
LANES = 128
MXU_DIM = 256
NEG_INF = -1e30
LOG2E = 1.4426950408889634

C_QA = 0
C_KVA = C_QA + Q_LORA
C_KR = C_KVA + KV_LORA
C_ZM = C_KR + 2 * QK_ROPE
C_FQ = C_ZM + D_MLA
C_FK = C_FQ + D_FOX
C_FV = C_FK + D_KV_FOX
C_ZF = C_FV + D_KV_FOX
C_FL = C_ZF + D_FOX
C_END = C_FL + LANES

VMEM_LIMIT = 56 * 1024 * 1024
N_SLOTS = 3


def _params(*sem):
    return pltpu.CompilerParams(dimension_semantics=sem, vmem_limit_bytes=VMEM_LIMIT)


def _const_spec(shape):
    nd = len(shape)
    return pl.BlockSpec(shape, lambda *_: (0,) * nd, pipeline_mode=pl.Buffered(1))


def _ada_kernel(c_ref, w_ref, b_ref, o_ref):
    c = c_ref[...]
    a = (c / (1.0 + jnp.exp(-c))).astype(BF16)
    o_ref[...] = jnp.dot(a, w_ref[...].astype(BF16), preferred_element_type=F32) + b_ref[...]


def _ada(c_all, w_ada, b_ada):
    rows, d = c_all.shape
    n = w_ada.shape[1]
    tn = 512
    return pl.pallas_call(
        _ada_kernel,
        grid=(n // tn,),
        in_specs=[pl.BlockSpec((rows, d), lambda j: (0, 0)),
                  pl.BlockSpec((d, tn), lambda j: (0, j)),
                  pl.BlockSpec((1, tn), lambda j: (0, j))],
        out_specs=pl.BlockSpec((rows, tn), lambda j: (0, j)),
        out_shape=jax.ShapeDtypeStruct((rows, n), F32),
        compiler_params=_params("arbitrary"),
        name="ada",
    )(c_all, w_ada, b_ada)


def _seg_rms(x, b_ref, gain):
    ms = jnp.dot((x * x).astype(BF16), b_ref[...], preferred_element_type=F32)
    return x * lax.rsqrt(ms + EPS) * gain


def _silu(z):
    return z / (1.0 + jnp.exp(-z))


def _proj_kernel(x_ref, shift_ref, scale_ref, gnorm_ref, win_ref, wuq_ref, wukv_ref,
                 gqa_ref, gq_ref, gkva_ref, gkn_ref, gkr_ref, gfq_ref, gfk_ref, bf_ref,
                 bq_ref, b128_ref, bkr_ref, tabq_ref, tabk_ref,
                 qcat_o, ckv_o, kr_o, kn_o, v_o, krb_o, fq_o, fk_o, fkb_o, fv_o, fvb_o, lf_o, sz_o):
    x = x_ref[...]
    ms = jnp.mean(x * x, axis=-1, keepdims=True)
    xn = x * lax.rsqrt(ms + EPS) * gnorm_ref[...]
    h = xn * (1.0 + scale_ref[...]) + shift_ref[...]
    hb = h.astype(BF16)

    def proj(lo, hi):
        return jnp.dot(hb, win_ref[:, lo:hi], preferred_element_type=F32)

    lane = lax.broadcasted_iota(jnp.int32, (x.shape[0], LANES), 1)

    qa = proj(C_QA, C_KVA)
    qn = qa * lax.rsqrt(jnp.mean(qa * qa, axis=-1, keepdims=True) + EPS) * gqa_ref[...]
    q_raw = jnp.dot(qn.astype(BF16), wuq_ref[...], preferred_element_type=F32)
    tabq = tabq_ref[...]
    for hd in range(H_MLA):
        lo = hd * QK_PAD
        y = _seg_rms(q_raw[:, lo:lo + QK_PAD], bq_ref, gq_ref[:, lo:lo + QK_PAD])
        t = y[:, LANES:] * tabq
        rot = t + pltpu.roll(t, QK_ROPE, axis=1)
        qcat_o[:, lo:lo + LANES] = y[:, :LANES].astype(BF16)
        qcat_o[:, lo + LANES:lo + QK_PAD] = jnp.where(lane < QK_ROPE, rot, 0.0).astype(BF16)

    kva = proj(C_KVA, C_KR)
    ckv = kva * lax.rsqrt(jnp.mean(kva * kva, axis=-1, keepdims=True) + EPS) * gkva_ref[...]
    ckv_o[...] = ckv
    kv_raw = jnp.dot(ckv.astype(BF16), wukv_ref[...], preferred_element_type=F32)
    for c in range(D_MLA // MXU_DIM):
        lo = c * MXU_DIM
        kn_o[:, lo:lo + MXU_DIM] = _seg_rms(kv_raw[:, lo:lo + MXU_DIM], b128_ref,
                                            gkn_ref[:, lo:lo + MXU_DIM]).astype(BF16)
    v_o[...] = kv_raw[:, D_MLA:].astype(BF16)

    kr = _seg_rms(proj(C_KR, C_ZM), bkr_ref, gkr_ref[...]) * tabk_ref[...]
    kr = kr + pltpu.roll(kr, QK_ROPE, axis=1)
    kr_o[...] = kr[:, :QK_ROPE]
    krb_o[...] = jnp.where(lane < QK_ROPE, kr, 0.0).astype(BF16)

    fq_raw = proj(C_FQ, C_FK)
    for c in range(D_FOX // MXU_DIM):
        lo = c * MXU_DIM
        fq_o[:, lo:lo + MXU_DIM] = _seg_rms(fq_raw[:, lo:lo + MXU_DIM], b128_ref,
                                            gfq_ref[:, lo:lo + MXU_DIM]).astype(BF16)
    fk = _seg_rms(proj(C_FK, C_FV), b128_ref, gfk_ref[...])
    fk_o[...] = fk
    fkb_o[...] = fk.astype(BF16)
    fv = proj(C_FV, C_ZF)
    fv_o[...] = fv
    fvb_o[...] = fv.astype(BF16)
    fl = proj(C_FL, C_END) + bf_ref[...]
    lf = jnp.minimum(fl, 0.0) - jnp.log(1.0 + jnp.exp(-jnp.abs(fl)))
    lf_o[...] = lf[:, :H_FOX]

    sz_o[:, :D_MLA] = _silu(proj(C_ZM, C_FQ)).astype(BF16)
    sz_o[:, D_MLA:] = _silu(proj(C_ZF, C_FL)).astype(BF16)


def _project(x2d, shift, scale, per_row_mod, tm, consts, tabq, tabk):
    rows, d = x2d.shape
    nt = rows // tm
    if per_row_mod:
        mod_spec = pl.BlockSpec((tm, d), lambda r: (r, 0))
    else:
        tiles_per_b = nt // shift.shape[0]
        mod_spec = pl.BlockSpec((None, 1, d), lambda r: (r // tiles_per_b, 0, 0))
    tab_tiles = tabq.shape[0] // tm
    row = lambda w: pl.BlockSpec((tm, w), lambda r: (r, 0))
    out_widths = [(H_MLA * QK_PAD, BF16), (KV_LORA, F32), (QK_ROPE, F32), (D_MLA, BF16), (D_MLA, BF16),
                  (LANES, BF16), (D_FOX, BF16), (D_KV_FOX, F32), (D_KV_FOX, BF16), (D_KV_FOX, F32),
                  (D_KV_FOX, BF16), (H_FOX, F32), (D_MLA + D_FOX, BF16)]
    in_specs = ([row(d), mod_spec, mod_spec] + [_const_spec(c.shape) for c in consts]
                + [pl.BlockSpec((tm, QK_PAD - LANES), lambda r: (r % tab_tiles, 0)),
                   pl.BlockSpec((tm, LANES), lambda r: (r % tab_tiles, 0))])
    return pl.pallas_call(
        _proj_kernel,
        grid=(nt,),
        in_specs=in_specs,
        out_specs=[row(w) for w, _ in out_widths],
        out_shape=[jax.ShapeDtypeStruct((rows, w), dt) for w, dt in out_widths],
        compiler_params=_params("arbitrary"),
        name="proj",
    )(x2d, shift, scale, *consts, tabq, tabk)


def _split3(x):
    hi = x.astype(BF16)
    r1 = x - hi.astype(F32)
    mid = r1.astype(BF16)
    lo = (r1 - mid.astype(F32)).astype(BF16)
    return hi, mid, lo


def _dot3(x, m):
    hi, mid, lo = _split3(x)
    d = lambda a: jnp.dot(a, m, preferred_element_type=F32)
    return d(hi) + (d(mid) + d(lo))


def _cumsum_kernel(x_ref, tri_ref, hi_ref, mid_ref, lo_ref):
    nblk = x_ref.shape[1] // LANES
    carry = jnp.zeros((x_ref.shape[0], 1), F32)
    for i in range(nblk):
        sl = slice(i * LANES, (i + 1) * LANES)
        blk = _dot3(x_ref[:, sl], tri_ref[...]) + carry
        carry = blk[:, LANES - 1:LANES]
        hi_ref[:, sl], mid_ref[:, sl], lo_ref[:, sl] = _split3(blk * LOG2E)


def _cumsum_lanes(x_t):
    b, hh, s = x_t.shape
    tri = jnp.asarray(np.triu(np.ones((LANES, LANES), np.float32)), BF16)
    spec = pl.BlockSpec((None, hh, s), lambda i: (i, 0, 0))
    return pl.pallas_call(
        _cumsum_kernel,
        grid=(b,),
        in_specs=[spec, _const_spec((LANES, LANES))],
        out_specs=[spec] * 3,
        out_shape=[jax.ShapeDtypeStruct((b, hh, s), BF16)] * 3,
        compiler_params=_params("arbitrary"),
        name="cumsum",
    )(x_t, tri)


def _fox_bias_lanes(f_terms):
    b, hh, s, nt = f_terms.shape
    w = hh * nt
    f = jnp.transpose(f_terms, (0, 2, 1, 3)).reshape(b, s, w)
    zeros = lambda k: jnp.zeros((b, s, k), BF16)
    qaux = jnp.concatenate([f, zeros(LANES - w)], axis=-1)
    kaux = jnp.concatenate([jnp.ones((b, s, w), BF16), -f, zeros(LANES - 2 * w)], axis=-1)
    own = np.repeat(np.eye(hh, dtype=np.float32), nt, axis=1)
    pad = np.zeros((hh, LANES - 2 * w), np.float32)
    qmask = jnp.asarray(np.concatenate([own, np.zeros_like(own), pad], axis=1), BF16)
    qones = jnp.asarray(np.concatenate([np.zeros_like(own), own, pad], axis=1), BF16)
    return qaux, kaux, qmask, qones


def _nt_dot(a, b):
    return lax.dot_general(a, b, (((1,), (1,)), ((), ())), preferred_element_type=F32)


def _p_attn_kernel(qcat_ref, kn_ref, krb_ref, v_ref, fq_ref, qaux_ref, fkb_ref, kaux_ref, fvb_ref, sz_ref,
                   qmask_ref, qones_ref, o_ref, m_m, acc_m, m_f, acc_f):
    qi = pl.program_id(1)
    ki = pl.program_id(2)
    tq = qcat_ref.shape[0]
    tk = kn_ref.shape[0]
    acc_w = V_DIM + LANES

    @pl.when(ki == 0)
    def _():
        m_m[...] = jnp.full_like(m_m, NEG_INF)
        m_f[...] = jnp.full_like(m_f, NEG_INF)
        acc_m[...] = jnp.zeros_like(acc_m)
        acc_f[...] = jnp.zeros_like(acc_f)

    def update(s, hd, m_ref, acc_ref, vv, ones):
        m_prev = m_ref[hd]
        m_new = jnp.maximum(m_prev, jnp.max(s, axis=-1, keepdims=True))
        alpha = jnp.exp2(m_prev - m_new)
        p = jnp.exp2(s - jnp.concatenate([m_new] * (tk // LANES), axis=1)).astype(BF16)
        pv = jnp.dot(p, jnp.concatenate([vv, ones], axis=1), preferred_element_type=F32)
        sl = slice(hd * acc_w, (hd + 1) * acc_w)
        acc_ref[:, sl] = jnp.concatenate([alpha] * (acc_w // LANES), axis=1) * acc_ref[:, sl] + pv
        m_ref[hd] = m_new

    def step(masked):
        if masked:
            keep = (lax.broadcasted_iota(jnp.int32, (tq, tk), 1)
                    <= lax.broadcasted_iota(jnp.int32, (tq, tk), 0))
        ones = jnp.ones((tk, LANES), BF16)
        krb = krb_ref[...]
        for hd in range(H_MLA):
            kcat = jnp.concatenate([kn_ref[:, hd * QK_NOPE:(hd + 1) * QK_NOPE], krb], axis=1)
            s = _nt_dot(qcat_ref[:, hd * QK_PAD:(hd + 1) * QK_PAD], kcat)
            if masked:
                s = jnp.where(keep, s, NEG_INF)
            update(s, hd, m_m, acc_m, v_ref[:, hd * V_DIM:(hd + 1) * V_DIM], ones)
        qaux = qaux_ref[...]
        kaux = kaux_ref[...]
        for hd in range(H_FOX):
            g = hd // FOX_GROUP
            hs = slice(hd * FOX_HD, (hd + 1) * FOX_HD)
            gs = slice(g * FOX_HD, (g + 1) * FOX_HD)
            qa = qaux * qmask_ref[hd:hd + 1, :] + qones_ref[hd:hd + 1, :]
            s = _nt_dot(jnp.concatenate([fq_ref[:, hs], qa], axis=1),
                        jnp.concatenate([fkb_ref[:, gs], kaux], axis=1))
            if masked:
                s = jnp.where(keep, s, NEG_INF)
            update(s, hd, m_f, acc_f, fvb_ref[:, gs], ones)

    @pl.when(ki < qi)
    def _():
        step(False)

    @pl.when(ki == qi)
    def _():
        step(True)
        for acc_ref, base in ((acc_m, 0), (acc_f, D_MLA)):
            for hd in range(H_MLA):
                lo = hd * acc_w
                so = slice(base + hd * V_DIM, base + (hd + 1) * V_DIM)
                o = acc_ref[:, lo:lo + V_DIM] / acc_ref[:, lo + V_DIM:lo + acc_w]
                o_ref[:, so] = (o * sz_ref[:, so].astype(F32)).astype(BF16)


def _prompt_attention(qcat, kn, krb, v, fq, qaux, fkb, kaux, fvb, sz, qmask, qones, tq):
    b, s, _ = qcat.shape
    nq = s // tq
    qmap = lambda bi, qi, ki: (bi, qi, 0)
    kmap = lambda bi, qi, ki: (bi, jnp.minimum(ki, qi), 0)
    qs = lambda w: pl.BlockSpec((None, tq, w), qmap)
    ks = lambda w: pl.BlockSpec((None, tq, w), kmap)
    acc_w = V_DIM + LANES
    return pl.pallas_call(
        _p_attn_kernel,
        grid=(b, nq, nq),
        in_specs=[qs(H_MLA * QK_PAD), ks(D_MLA), ks(LANES), ks(D_MLA), qs(D_FOX), qs(LANES),
                  ks(D_KV_FOX), ks(LANES), ks(D_KV_FOX), qs(D_MLA + D_FOX),
                  pl.BlockSpec((H_FOX, LANES), lambda bi, qi, ki: (0, 0)),
                  pl.BlockSpec((H_FOX, LANES), lambda bi, qi, ki: (0, 0))],
        out_specs=qs(D_MLA + D_FOX),
        out_shape=jax.ShapeDtypeStruct((b, s, D_MLA + D_FOX), BF16),
        scratch_shapes=[pltpu.VMEM((H_MLA, tq, LANES), F32), pltpu.VMEM((tq, H_MLA * acc_w), F32),
                        pltpu.VMEM((H_FOX, tq, LANES), F32), pltpu.VMEM((tq, H_FOX * acc_w), F32)],
        compiler_params=_params("arbitrary", "arbitrary", "arbitrary"),
        name="p_attn",
    )(qcat, kn, krb, v, fq, qaux, fkb, kaux, fvb, sz, qmask, qones)


def _out_kernel(x_ref, gate_ref, merged_ref, w_ref, o_ref):
    y = jnp.dot(merged_ref[...], w_ref[...], preferred_element_type=F32)
    o_ref[...] = x_ref[...] + gate_ref[...] * y


def _out_project(x2d, gate, merged, w_out_b, tm):
    rows, d = x2d.shape
    nt = rows // tm
    tiles_per_b = nt // gate.shape[0]
    return pl.pallas_call(
        _out_kernel,
        grid=(nt,),
        in_specs=[pl.BlockSpec((tm, d), lambda r: (r, 0)),
                  pl.BlockSpec((None, 1, d), lambda r: (r // tiles_per_b, 0, 0)),
                  pl.BlockSpec((tm, merged.shape[1]), lambda r: (r, 0)),
                  _const_spec(w_out_b.shape)],
        out_specs=pl.BlockSpec((tm, d), lambda r: (r, 0)),
        out_shape=jax.ShapeDtypeStruct((rows, d), F32),
        compiler_params=_params("arbitrary"),
        name="out_proj",
    )(x2d, gate, merged, w_out_b)


def _absorb_kernel(qcat_ref, gkn_ref, wukt_ref, o_ref):
    for hd in range(H_MLA):
        qn = qcat_ref[:, hd * QK_PAD:hd * QK_PAD + QK_NOPE].astype(F32) * gkn_ref[...]
        o_ref[hd] = jnp.dot(qn.astype(BF16), wukt_ref[hd * QK_NOPE:(hd + 1) * QK_NOPE, :],
                            preferred_element_type=F32).astype(BF16)


def _absorb(qcat_s, gkn, wukt):
    n = qcat_s.shape[0]
    return pl.pallas_call(
        _absorb_kernel,
        grid=(1,),
        in_specs=[_const_spec(qcat_s.shape), _const_spec(gkn.shape), _const_spec(wukt.shape)],
        out_specs=_const_spec((H_MLA, n, KV_LORA)),
        out_shape=jax.ShapeDtypeStruct((H_MLA, n, KV_LORA), BF16),
        compiler_params=_params("arbitrary"),
        name="absorb",
    )(qcat_s, gkn, wukt)


def _s_attn_kernel(pt_ref, qt_ref, qr_ref, fq_ref, lfn_ref, wukt_ref, sfx_ref,
                   ckv_hbm, kr_hbm, fk_hbm, fv_hbm, lf_hbm,
                   mm_o, lm_o, am_o, mf_o, lf_o, af_o,
                   cbuf, kbuf, fkbuf, fvbuf, lbuf, sems, cb, krb, fkb, fvb, carry,
                   *, layer, pages, page, nc, n):
    i = pl.program_id(0)
    c = pl.program_id(1)
    g = i * nc + c
    total = n * nc
    slot = lax.rem(g, N_SLOTS)

    def chunk_of(step):
        wrapped = lax.rem(step, total)
        return lax.div(wrapped, nc), lax.rem(wrapped, nc), lax.rem(step, N_SLOTS)
    t = pages * page
    rows_kv = page * FOX_KV_HEADS

    def page_copies(seq, chunk, slot_):
        cps = []
        for j in range(pages):
            pid = pt_ref[seq, (nc - 1 - chunk) * pages + j]
            rows = pl.ds(j * page, page)
            rows2 = pl.ds(j * rows_kv, rows_kv)
            for k, (src, dst) in enumerate(((ckv_hbm, cbuf.at[slot_, rows]), (kr_hbm, kbuf.at[slot_, :, rows]),
                                            (fk_hbm, fkbuf.at[slot_, rows2]), (fv_hbm, fvbuf.at[slot_, rows2]),
                                            (lf_hbm, lbuf.at[slot_, pl.ds(j * H_FOX, H_FOX)]))):
                cps.append(pltpu.make_async_copy(src.at[layer, pid], dst, sems.at[k, slot_]))
        return cps

    @pl.when(g == 0)
    def _():
        krb[...] = jnp.zeros_like(krb)
        for ahead in range(N_SLOTS - 1):
            for cp in page_copies(ahead // nc, ahead % nc, ahead % N_SLOTS):
                cp.start()

    @pl.when(c == 0)
    def _():
        mm_o[...] = jnp.full_like(mm_o, NEG_INF)
        mf_o[...] = jnp.full_like(mf_o, NEG_INF)
        lm_o[...] = jnp.zeros_like(lm_o)
        lf_o[...] = jnp.zeros_like(lf_o)
        am_o[...] = jnp.zeros_like(am_o)
        af_o[...] = jnp.zeros_like(af_o)
        carry[...] = lfn_ref[...] * LOG2E

    for cp in page_copies(i, c, slot):
        cp.wait()
    nxt = page_copies(*chunk_of(g + (N_SLOTS - 1)))
    per_page = len(nxt) // pages
    for j in range(pages):
        for cp in nxt[j * per_page:(j + 1) * per_page]:
            cp.start()
        rows = slice(j * page, (j + 1) * page)
        cb[rows, :] = cbuf[slot, rows, :].astype(BF16)
        krb[0:QK_ROPE, rows] = kbuf[slot, :, rows].astype(BF16)
        for gg in range(FOX_KV_HEADS):
            src_rows = pl.ds(j * rows_kv + gg, page, stride=FOX_KV_HEADS)
            fkb[gg, rows, :] = fkbuf[slot, src_rows, :].astype(BF16)
            fvb[gg, rows, :] = fvbuf[slot, src_rows, :].astype(BF16)

    def update(s, m_o, l_o):
        m_prev = m_o[...]
        m_new = jnp.maximum(m_prev, jnp.max(s, axis=-1, keepdims=True))
        alpha = jnp.exp2(m_prev - m_new)
        p = jnp.exp2(s - jnp.concatenate([m_new] * (t // LANES), axis=1))
        l_o[...] = alpha * l_o[...] + jnp.sum(p, axis=-1, keepdims=True)
        m_o[...] = m_new
        return jnp.concatenate([alpha, alpha], axis=1), p.astype(BF16)

    lhs = jnp.concatenate([wukt_ref[...], qt_ref[...]], axis=0)
    sub = min(t, 512)
    n_sub = t // sub

    def mla_logits(u):
        kt = _nt_dot(lhs, cb[u * sub:(u + 1) * sub, :])
        ssq = jnp.concatenate(
            [jnp.sum(kt[hd * QK_NOPE:(hd + 1) * QK_NOPE, :] ** 2, axis=0, keepdims=True) for hd in range(H_MLA)],
            axis=0)
        return kt[H_MLA * QK_NOPE:H_MLA * QK_NOPE + H_MLA, :] * lax.rsqrt(ssq * (1.0 / QK_NOPE) + EPS)

    def fox_logits():
        sfx = _dot3(lbuf[slot] * LOG2E, sfx_ref[...])
        run = carry[...]
        bias = [None] * pages
        for j in reversed(range(pages)):
            blk = sfx[j * H_FOX:(j + 1) * H_FOX, :]
            bias[j] = blk[:, :LANES] + run
            run = run + blk[:, LANES:]
        carry[...] = run
        fqv = fq_ref[...]
        s_f = jnp.concatenate(bias, axis=1)
        for gg in range(FOX_KV_HEADS):
            s_f = s_f + _nt_dot(fqv[:, gg * FOX_HD:(gg + 1) * FOX_HD], fkb[gg])
        return s_f

    def fox_values(alpha, p):
        pv = jnp.concatenate([jnp.dot(p, fvb[gg], preferred_element_type=F32) for gg in range(FOX_KV_HEADS)],
                             axis=1)
        af_o[...] = alpha * af_o[...] + pv

    s_m = []
    fox_p = None
    logits_at, values_at = min(1, n_sub - 1), min(2, n_sub - 1)
    for u in range(n_sub):
        s_m.append(mla_logits(u))
        if u == logits_at:
            fox_p = update(fox_logits(), mf_o, lf_o)
        if u == values_at:
            fox_values(*fox_p)
    s_m = jnp.concatenate(s_m, axis=1) + jnp.dot(qr_ref[...], krb[...], preferred_element_type=F32)
    alpha, p = update(s_m, mm_o, lm_o)
    am_o[...] = alpha * am_o[...] + jnp.dot(p, cb[...], preferred_element_type=F32)

    @pl.when(g == total - 1)
    def _():
        for ahead in range(1, N_SLOTS):
            for cp in page_copies(*chunk_of(g + ahead)):
                cp.wait()


def _sample_attention(layer, page_table, qt, qr, fqb, lfn, wukt, ckv_c, kr_c, fk_c, fv_c, lf_c, pages):
    n, n_pages = page_table.shape
    page = ckv_c.shape[2]
    assert page == LANES
    nc = n_pages // pages
    t = pages * page
    idx = np.arange(page)
    sfx = np.concatenate([(idx[:, None] > idx[None, :]).astype(np.float32),
                          np.ones((page, LANES), np.float32)], axis=1)
    sfx = jnp.asarray(sfx, BF16)

    seq = lambda shape: pl.BlockSpec((None,) + shape, lambda i, c, pt: (i,) + (0,) * len(shape))
    cst = lambda shape: pl.BlockSpec(shape, lambda i, c, pt: (0,) * len(shape))
    hbm = pl.BlockSpec(memory_space=pl.ANY)
    in_specs = [seq((2 * H_MLA, KV_LORA)), seq((H_MLA, LANES)), seq((H_FOX, D_KV_FOX)), seq((H_FOX, LANES)),
                cst(wukt.shape), cst(sfx.shape), hbm, hbm, hbm, hbm, hbm]
    stat = jax.ShapeDtypeStruct((n, H_MLA, LANES), F32)
    accs = jax.ShapeDtypeStruct((n, H_MLA, KV_LORA), F32)
    n_arrays = 5
    grid_spec = pltpu.PrefetchScalarGridSpec(
        num_scalar_prefetch=1,
        grid=(n, nc),
        in_specs=in_specs,
        out_specs=[seq((H_MLA, LANES)), seq((H_MLA, LANES)), seq((H_MLA, KV_LORA)),
                   seq((H_FOX, LANES)), seq((H_FOX, LANES)), seq((H_FOX, D_KV_FOX))],
        scratch_shapes=[pltpu.VMEM((N_SLOTS, t, KV_LORA), F32), pltpu.VMEM((N_SLOTS, QK_ROPE, t), F32),
                        pltpu.VMEM((N_SLOTS, t * FOX_KV_HEADS, FOX_HD), F32),
                        pltpu.VMEM((N_SLOTS, t * FOX_KV_HEADS, FOX_HD), F32),
                        pltpu.VMEM((N_SLOTS, pages * H_FOX, page), F32),
                        pltpu.SemaphoreType.DMA((n_arrays, N_SLOTS)),
                        pltpu.VMEM((t, KV_LORA), BF16), pltpu.VMEM((LANES, t), BF16),
                        pltpu.VMEM((FOX_KV_HEADS, t, FOX_HD), BF16), pltpu.VMEM((FOX_KV_HEADS, t, FOX_HD), BF16),
                        pltpu.VMEM((H_FOX, LANES), F32)],
    )
    return pl.pallas_call(
        functools.partial(_s_attn_kernel, layer=layer, pages=pages, page=page, nc=nc, n=n),
        grid_spec=grid_spec,
        out_shape=[stat, stat, accs, stat, stat, accs],
        compiler_params=_params("arbitrary", "arbitrary"),
        name="s_attn",
    )(page_table, qt, qr, fqb, lfn, wukt, sfx, ckv_c, kr_c, fk_c, fv_c, lf_c)


def _finish_kernel(x_ref, gate_ref, qcat_ref, kn_ref, krb_ref, ckv_ref, fq_ref, fkb_ref, fvb_ref, sz_ref,
                   mm_ref, lm_ref, am_ref, mf_ref, lf_ref, af_ref, wuv_ref, wout_ref, o_ref, merged):
    krb = krb_ref[...].astype(F32)
    cself = ckv_ref[...].astype(BF16).astype(F32)

    def merge(s_self, m_past, l_past):
        m_tot = jnp.maximum(m_past, s_self)
        a = jnp.exp2(m_past - m_tot)
        b = jnp.exp2(s_self - m_tot)
        inv = 1.0 / (a * l_past + b)
        return a * inv, b * inv

    for hd in range(H_MLA):
        q = qcat_ref[:, hd * QK_PAD:(hd + 1) * QK_PAD].astype(F32)
        s_self = (jnp.sum(q[:, :QK_NOPE] * kn_ref[:, hd * QK_NOPE:(hd + 1) * QK_NOPE].astype(F32),
                          axis=-1, keepdims=True)
                  + jnp.sum(q[:, QK_NOPE:] * krb, axis=-1, keepdims=True))
        wa, wb = merge(s_self, mm_ref[hd], lm_ref[hd])
        lat = wa * am_ref[hd] + wb * cself
        o = jnp.dot(lat.astype(BF16), wuv_ref[:, hd * V_DIM:(hd + 1) * V_DIM], preferred_element_type=F32)
        sl = slice(hd * V_DIM, (hd + 1) * V_DIM)
        merged[:, sl] = (o * sz_ref[:, sl].astype(F32)).astype(BF16)
    for hd in range(H_FOX):
        g = hd // FOX_GROUP
        gs = slice(g * FOX_HD, (g + 1) * FOX_HD)
        s_self = jnp.sum(fq_ref[:, hd * FOX_HD:(hd + 1) * FOX_HD].astype(F32) * fkb_ref[:, gs].astype(F32),
                         axis=-1, keepdims=True)
        wa, wb = merge(s_self, mf_ref[hd], lf_ref[hd])
        o = wa * af_ref[hd][:, gs] + wb * fvb_ref[:, gs].astype(F32)
        so = slice(D_MLA + hd * FOX_HD, D_MLA + (hd + 1) * FOX_HD)
        merged[:, so] = (o * sz_ref[:, so].astype(F32)).astype(BF16)
    y = jnp.dot(merged[...], wout_ref[...], preferred_element_type=F32)
    o_ref[...] = x_ref[...] + gate_ref[...] * y


def _finish(*args):
    x = args[0]
    return pl.pallas_call(
        _finish_kernel,
        grid=(1,),
        in_specs=[_const_spec(a.shape) for a in args],
        out_specs=_const_spec(x.shape),
        out_shape=jax.ShapeDtypeStruct(x.shape, F32),
        scratch_shapes=[pltpu.VMEM((x.shape[0], D_MLA + D_FOX), BF16)],
        compiler_params=_params("arbitrary"),
        name="finish",
    )(*args)


def _swap_halves(a, axis=-1):
    lo, hi = jnp.split(a, 2, axis=axis)
    return jnp.concatenate([hi, lo], axis=axis)


def _seg_matrix(widths):
    n = sum(widths)
    m = np.zeros((n, n), np.float32)
    start = 0
    for w in widths:
        m[start:start + w, start:start + w] = 1.0 / w
        start += w
    return jnp.asarray(m, BF16)


def _rope_tables(pos):
    inv_freq = ROPE_THETA ** (-jnp.arange(0, QK_ROPE, 2, dtype=F32) / QK_ROPE)
    ang = pos.astype(F32)[:, None] * inv_freq[None, :]
    cos, sin = jnp.cos(ang), jnp.sin(ang)
    return jnp.concatenate([cos, cos, -sin, sin], axis=-1)


def _layer_consts(l, g_norm, w_in, b_f, g_qa, w_uq, g_qn_nope, g_qn_rope, g_kva, g_kn_nope, g_kn_rope,
                  w_uk, w_uv, g_fq, g_fk):
    w = w_in[l]
    d = w.shape[0]
    widths = (Q_LORA, KV_LORA, QK_ROPE, D_MLA, D_FOX, D_KV_FOX, D_KV_FOX, H_FOX, D_FOX)
    offs = np.concatenate([[0], np.cumsum(widths)])
    part = lambda i: w[:, offs[i]:offs[i + 1]]
    w_aug = jnp.concatenate([part(0), part(1), part(2), _swap_halves(part(2)), part(3), part(4), part(5),
                             part(6), part(8), part(7), jnp.zeros((d, LANES - H_FOX), F32)], axis=1).astype(BF16)
    uq = w_uq[l].reshape(Q_LORA, H_MLA, QK_HEAD)
    uq_rope = uq[:, :, QK_NOPE:]
    wuq = jnp.concatenate([uq[:, :, :QK_NOPE], uq_rope, _swap_halves(uq_rope)], axis=-1)
    wuq = wuq.reshape(Q_LORA, H_MLA * QK_PAD).astype(BF16)
    wuk = w_uk[l].reshape(KV_LORA, D_MLA)
    wuv = w_uv[l].reshape(KV_LORA, D_MLA)
    wukv = jnp.concatenate([wuk, wuv], axis=1).astype(BF16)
    gq = jnp.concatenate([g_qn_nope[l], g_qn_rope[l], _swap_halves(g_qn_rope[l])]) * (MLA_SCALE * LOG2E)
    row = lambda a: a.reshape(1, -1).astype(F32)
    consts = [row(g_norm[l]), w_aug, wuq, wukv, row(g_qa[l]), row(jnp.tile(gq, H_MLA)), row(g_kva[l]),
              row(jnp.tile(g_kn_nope[l], H_MLA)),
              row(jnp.concatenate([g_kn_rope[l], _swap_halves(g_kn_rope[l])])),
              row(jnp.tile(g_fq[l], H_FOX) * (FOX_SCALE * LOG2E)), row(jnp.tile(g_fk[l], FOX_KV_HEADS)),
              row(jnp.concatenate([b_f[l], jnp.zeros((LANES - H_FOX,), F32)])),
              _seg_matrix((QK_NOPE, QK_ROPE, QK_ROPE)), _seg_matrix((LANES, LANES)),
              _seg_matrix((QK_ROPE, QK_ROPE))]
    return consts, wuk.T.astype(BF16), wuv.astype(BF16), row(g_kn_nope[l])


def kernel(x_prompt, x_sample, c_prompt, c_sample, cache_mla_ckv, cache_mla_krope, cache_fox_k, cache_fox_v, cache_fox_logf, page_table, g_norm, w_ada, b_ada, w_in, b_f, g_qa, w_uq, g_qn_nope, g_qn_rope, g_kva, g_kn_nope, g_kn_rope, w_uk, w_uv, g_fq, g_fk, w_out):
    b, s, d = x_prompt.shape
    n, t_new, _ = x_sample.shape
    depth, n_pool, page = cache_mla_ckv.shape[:3]
    n_pages = page_table.shape[1]
    past = n_pages * page
    assert t_new == 1 and n % 8 == 0 and s % LANES == 0
    tm = min(256, s)
    tq = min(512, s)
    pages_per_step = min(16, n_pages)
    pad_rows = (-(b + n)) % 8

    tab_p = _rope_tables(jnp.arange(s))
    tab_s = jnp.broadcast_to(_rope_tables(past + jnp.arange(t_new)), (n, 2 * QK_ROPE))

    xp, xs = x_prompt, x_sample
    outs_p, outs_s = [], []
    for l in range(depth):
        consts, wukt, wuv_b, gkn_row = _layer_consts(l, g_norm, w_in, b_f, g_qa, w_uq, g_qn_nope, g_qn_rope,
                                                      g_kva, g_kn_nope, g_kn_rope, w_uk, w_uv, g_fq, g_fk)
        w_out_b = w_out[l].astype(BF16)

        c_all = jnp.concatenate([c_prompt, c_sample, jnp.zeros((pad_rows, d), F32)], axis=0)
        mod = _ada(c_all, w_ada[l], b_ada[l].reshape(1, -1))
        shift_p, scale_p, gate_p = (m.reshape(b, 1, d) for m in jnp.split(mod[:b], 3, axis=-1))
        shift_s, scale_s, gate_s = jnp.split(mod[b:b + n], 3, axis=-1)

        (qcat, ckv, kr, kn, v, krb, fq, fk, fkb, fv, fvb, lf, sz) = _project(
            xp.reshape(b * s, d), shift_p, scale_p, False, tm, consts, tab_p, tab_p)
        lf3 = lf.reshape(b, s, H_FOX)
        f_terms = jnp.stack(_cumsum_lanes(jnp.swapaxes(lf3, 1, 2)), axis=-1)
        qaux, kaux, qmask, qones = _fox_bias_lanes(f_terms)
        r3 = lambda a: a.reshape(b, s, a.shape[-1])
        merged = _prompt_attention(r3(qcat), r3(kn), r3(krb), r3(v), r3(fq), qaux, r3(fkb), kaux, r3(fvb),
                                   r3(sz), qmask, qones, tq)
        xp = _out_project(xp.reshape(b * s, d), gate_p, merged.reshape(b * s, -1), w_out_b, tq).reshape(b, s, d)
        outs_p.append((ckv.reshape(b, s, KV_LORA), kr.reshape(b, s, QK_ROPE),
                       fk.reshape(b, s, FOX_KV_HEADS, FOX_HD), fv.reshape(b, s, FOX_KV_HEADS, FOX_HD), lf3))

        (qcat_s, ckv_s, kr_s, kn_s, v_s, krb_s, fq_s, fk_s, fkb_s, fv_s, fvb_s, lf_s, sz_s) = _project(
            xs.reshape(n, d), shift_s, scale_s, True, n, consts, tab_s, tab_s)
        qt = jnp.swapaxes(_absorb(qcat_s, gkn_row, wukt), 0, 1)
        qt = jnp.concatenate([qt, jnp.zeros_like(qt)], axis=1)
        q3 = qcat_s.reshape(n, H_MLA, QK_PAD)
        qr = q3[:, :, QK_NOPE:]
        f3 = fq_s.reshape(n, FOX_KV_HEADS, FOX_GROUP, FOX_HD)
        eye = jnp.eye(FOX_KV_HEADS, dtype=BF16)
        fqb = jnp.einsum('ngjd,gk->ngjkd', f3, eye).reshape(n, H_FOX, D_KV_FOX)
        lfn = jnp.broadcast_to(lf_s[:, :, None], (n, H_FOX, LANES))
        fk_c = cache_fox_k.reshape(depth, n_pool, page * FOX_KV_HEADS, FOX_HD)
        fv_c = cache_fox_v.reshape(depth, n_pool, page * FOX_KV_HEADS, FOX_HD)
        kr_c = jnp.swapaxes(cache_mla_krope, 2, 3)
        lf_c = jnp.swapaxes(cache_fox_logf, 2, 3)
        stats = _sample_attention(l, page_table, qt, qr, fqb, lfn, wukt, cache_mla_ckv, kr_c,
                                  fk_c, fv_c, lf_c, pages_per_step)
        mm, lm, am, mf, lff, af = stats
        hn = lambda a: jnp.swapaxes(a, 0, 1)
        xs = _finish(xs.reshape(n, d), gate_s, qcat_s, kn_s, krb_s, ckv_s, fq_s, fkb_s, fvb_s, sz_s,
                     hn(mm)[:, :, :1], hn(lm)[:, :, :1], hn(am), hn(mf)[:, :, :1], hn(lff)[:, :, :1], hn(af),
                     wuv_b, w_out_b).reshape(n, t_new, d)
        outs_s.append((ckv_s.reshape(n, t_new, KV_LORA), kr_s.reshape(n, t_new, QK_ROPE),
                       fk_s.reshape(n, t_new, FOX_KV_HEADS, FOX_HD), fv_s.reshape(n, t_new, FOX_KV_HEADS, FOX_HD),
                       lf_s.reshape(n, t_new, H_FOX)))

    stack = lambda outs, i: jnp.stack([o[i] for o in outs])
    return (xp, xs, *(stack(outs_p, i) for i in range(5)), *(stack(outs_s, i) for i in range(5)))
```
